```python
import jax
import jax.numpy as jnp
from jax import lax
import numpy as np

D_MODEL = 1024
BATCH = 8
SEQ = 2048
DEPTH = 1
DEC_BATCH = 128
DEC_SEQ = 1
PAST_LEN = 8192
PAGE_SIZE = 128

EPS = 1e-6
POOL_WINDOWS = (2, 4, 8, 16)
N_POOL_GROUPS = len(POOL_WINDOWS)
D_POOL = D_MODEL
POOL_GROUP = D_POOL // N_POOL_GROUPS
POOL_HIST = max(POOL_WINDOWS) - 1
ATT_GROUPS = ((128, 1), (512, 4), (2048, 16))
N_ATT_GROUPS = len(ATT_GROUPS)
N_SLOTS = 8
HEAD_DIM = 64
N_HEADS = N_ATT_GROUPS * N_SLOTS
D_ATT = N_HEADS * HEAD_DIM
D_ATT_OUT = N_SLOTS * HEAD_DIM
D_FF = 2816
CONV_W = 3
D_IN = D_POOL + 3 * D_ATT + 2 * D_MODEL
IN_SPLITS = (D_POOL, D_POOL + D_ATT, D_POOL + 2 * D_ATT, D_POOL + 3 * D_ATT, D_POOL + 3 * D_ATT + D_MODEL)

kernel_name = 'hybrid_pool_dilated_attn_convffn_step'


def rms_norm(x, g):
    xf = x.astype(jnp.float32)
    y = xf * lax.rsqrt(jnp.mean(xf * xf, axis=-1, keepdims=True) + EPS)
    return (y * g.astype(jnp.float32)).astype(x.dtype)


def modulate(x, shift, scale):
    return x * (1 + scale) + shift


def pool_mix(u_ext, pos, w_map, scale):
    B, n_ext, _ = u_ext.shape
    T = pos.shape[0]
    P = n_ext - T
    uf = u_ext.astype(jnp.float32)
    cs = jnp.concatenate([jnp.zeros_like(uf[:, :1]), jnp.cumsum(uf, axis=1)], axis=1)
    cs_end = cs[:, P + 1:]
    u_new = uf[:, P:]
    groups = []
    for gi, win in enumerate(POOL_WINDOWS):
        lo, hi = gi * POOL_GROUP, (gi + 1) * POOL_GROUP
        window_sum = cs_end[..., lo:hi] - cs[:, P + 1 - win:P + 1 - win + T, lo:hi]
        count = jnp.minimum(pos + 1, win).astype(jnp.float32)[None, :, None]
        groups.append(window_sum / count - u_new[..., lo:hi])
    pooled = jnp.stack(groups, axis=2)
    mixed = jnp.einsum('btgc,gcd->btgd', pooled, w_map.astype(jnp.float32)).reshape(B, T, D_POOL)
    return (mixed * scale.astype(jnp.float32)).astype(u_ext.dtype)


def strided_window_attn(q, k, v, dil, span):
    B, T, S, Dh = q.shape
    n_sub = T // dil
    nb = -(-n_sub // span)
    pad = nb * span - n_sub

    def to_blocks(a):
        a = a.reshape(B, n_sub, dil, S, Dh).transpose(0, 2, 1, 3, 4)
        a = jnp.pad(a, ((0, 0), (0, 0), (0, pad), (0, 0), (0, 0)))
        return a.reshape(B, dil, nb, span, S, Dh)

    def with_prev(a):
        prev = jnp.pad(a[:, :, :-1], ((0, 0), (0, 0), (1, 0), (0, 0), (0, 0), (0, 0)))
        return jnp.concatenate([prev, a], axis=3)

    qb, kb, vb = to_blocks(q), to_blocks(k), to_blocks(v)
    kc, vc = with_prev(kb), with_prev(vb)
    s = jnp.einsum('brnqsd,brnksd->brnsqk', qb.astype(jnp.float32), kc.astype(jnp.float32)) * (HEAD_DIM ** -0.5)
    qi = jnp.arange(span)[:, None] + span
    ki = jnp.arange(2 * span)[None, :]
    dist = qi - ki
    band = (dist >= 0) & (dist <= span)
    no_prev = (jnp.arange(nb) == 0)[:, None, None] & (ki < span)[None]
    mask = band[None] & ~no_prev
    s = jnp.where(mask[None, None, :, None], s, -jnp.inf)
    m = jnp.max(s, axis=-1, keepdims=True)
    p = jnp.exp(s - m)
    den = jnp.sum(p, axis=-1, keepdims=True)
    den_t = den[..., 0].transpose(0, 1, 2, 4, 3)
    o = jnp.einsum('brnsqk,brnksd->brnqsd', p, vc.astype(jnp.float32)) / den_t[..., None]
    lse = (m[..., 0] + jnp.log(den[..., 0])).transpose(0, 1, 2, 4, 3)
    o = o.reshape(B, dil, nb * span, S, Dh)[:, :, :n_sub].transpose(0, 2, 1, 3, 4).reshape(B, T, S, Dh)
    lse = lse.reshape(B, dil, nb * span, S)[:, :, :n_sub].transpose(0, 2, 1, 3).reshape(B, T, S)
    return o, lse


def strided_window_attn_step(q, k_ext, v_ext, dil, span):
    B, T, S, Dh = q.shape
    L = k_ext.shape[1] - T
    ki = (L + jnp.arange(T))[:, None] - dil * jnp.arange(span + 1)[None, :]
    valid = ki >= 0
    idx = jnp.maximum(ki, 0)
    kg = k_ext[:, idx].astype(jnp.float32)
    vg = v_ext[:, idx].astype(jnp.float32)
    s = jnp.einsum('btsd,btjsd->btsj', q.astype(jnp.float32), kg) * (HEAD_DIM ** -0.5)
    s = jnp.where(valid[None, :, None, :], s, -jnp.inf)
    m = jnp.max(s, axis=-1, keepdims=True)
    p = jnp.exp(s - m)
    den = jnp.sum(p, axis=-1, keepdims=True)
    o = jnp.einsum('btsj,btjsd->btsd', p, vg) / den
    lse = m[..., 0] + jnp.log(den[..., 0])
    return o, lse


def merge_groups(outs, lses):
    o = jnp.stack(outs, axis=2)
    w = jax.nn.softmax(jnp.stack(lses, axis=2), axis=2)
    return jnp.sum(o * w[..., None], axis=2)


def hybrid_layer(x, c, pos, pool_hist, conv_hist, kv_hist, prm, prompt):
    (w_ada, b_ada, g_pre_mix, g_post_mix, g_pre_ffn, g_post_ffn, w_in, w_pool_map, pool_scale,
     w_proj_pool, w_proj_att, w_out, w_up, w_conv, b_conv, w_down) = prm
    B, T, _ = x.shape
    mod = (jax.nn.silu(c) @ w_ada + b_ada)[:, None, :]
    shift_m, scale_m, gate_m, shift_f, scale_f, gate_f = jnp.split(mod, 6, axis=-1)

    h = modulate(rms_norm(x, g_pre_mix), shift_m, scale_m)
    u, q, k, v, z_pool, z_att = jnp.split(h @ w_in, IN_SPLITS, axis=-1)
    u_ext = jnp.concatenate([pool_hist, u], axis=1)
    y_pool = pool_mix(u_ext, pos, w_pool_map, pool_scale) @ w_proj_pool
    q = q.reshape(B, T, N_ATT_GROUPS, N_SLOTS, HEAD_DIM)
    k = k.reshape(B, T, N_ATT_GROUPS, N_SLOTS, HEAD_DIM)
    v = v.reshape(B, T, N_ATT_GROUPS, N_SLOTS, HEAD_DIM)
    outs, lses, new_kv = [], [], []
    for gi, (win, dil) in enumerate(ATT_GROUPS):
        span = win // dil
        q_g, k_g, v_g = q[:, :, gi], k[:, :, gi], v[:, :, gi]
        if prompt:
            k_ext, v_ext = k_g, v_g
            o_g, lse_g = strided_window_attn(q_g, k_g, v_g, dil, span)
        else:
            k_hist, v_hist = kv_hist[gi]
            k_ext = jnp.concatenate([k_hist, k_g], axis=1)
            v_ext = jnp.concatenate([v_hist, v_g], axis=1)
            o_g, lse_g = strided_window_attn_step(q_g, k_ext, v_ext, dil, span)
        keep = min(win, k_ext.shape[1])
        new_kv.append(k_ext[:, -keep:])
        new_kv.append(v_ext[:, -keep:])
        outs.append(o_g)
        lses.append(lse_g)
    o = merge_groups(outs, lses).reshape(B, T, D_ATT_OUT).astype(x.dtype)
    y_att = o @ w_proj_att
    mix = (jax.nn.sigmoid(z_pool) * y_pool + jax.nn.sigmoid(z_att) * y_att) @ w_out
    x = x + gate_m * rms_norm(mix, g_post_mix)

    h = modulate(rms_norm(x, g_pre_ffn), shift_f, scale_f)
    hu_ext = jnp.concatenate([conv_hist, h @ w_up], axis=1)
    hc = b_conv
    for j in range(CONV_W):
        hc = hc + hu_ext[:, j:j + T] * w_conv[j]
    a, b = jnp.split(hc, 2, axis=-1)
    ffn = (jax.nn.gelu(a, approximate=True) * b) @ w_down
    x = x + gate_f * rms_norm(ffn, g_post_ffn)
    new_state = (new_kv[0], new_kv[1], new_kv[2], new_kv[3], new_kv[4], new_kv[5],
                 u_ext[:, -POOL_HIST:], hu_ext[:, -(CONV_W - 1):])
    return x, new_state


def setup_inputs(seed: int = 0) -> dict:
    key = jax.random.key(seed)
    ks = jax.random.split(key, 28)
    f32 = jnp.float32

    def nrm(k, shape, s=1.0):
        return jax.random.normal(k, shape, f32) * s

    def gain(k):
        return 1.0 + nrm(k, (DEPTH, D_MODEL), 0.05)

    L = [min(win, PAST_LEN) for win, _ in ATT_GROUPS]
    return {
        'x_prompt': nrm(ks[0], (BATCH, SEQ, D_MODEL)),
        'x_sample': nrm(ks[1], (DEC_BATCH, DEC_SEQ, D_MODEL)),
        'c_prompt': nrm(ks[2], (BATCH, D_MODEL)),
        'c_sample': nrm(ks[3], (DEC_BATCH, D_MODEL)),
        'cache_k_w128': nrm(ks[4], (DEPTH, DEC_BATCH, L[0], N_SLOTS, HEAD_DIM)),
        'cache_v_w128': nrm(ks[5], (DEPTH, DEC_BATCH, L[0], N_SLOTS, HEAD_DIM)),
        'cache_k_w512': nrm(ks[6], (DEPTH, DEC_BATCH, L[1], N_SLOTS, HEAD_DIM)),
        'cache_v_w512': nrm(ks[7], (DEPTH, DEC_BATCH, L[1], N_SLOTS, HEAD_DIM)),
        'cache_k_w2048': nrm(ks[8], (DEPTH, DEC_BATCH, L[2], N_SLOTS, HEAD_DIM)),
        'cache_v_w2048': nrm(ks[9], (DEPTH, DEC_BATCH, L[2], N_SLOTS, HEAD_DIM)),
        'state_pool': nrm(ks[10], (DEPTH, DEC_BATCH, POOL_HIST, D_POOL)),
        'state_conv': nrm(ks[11], (DEPTH, DEC_BATCH, CONV_W - 1, 2 * D_FF)),
        'w_ada': nrm(ks[12], (DEPTH, D_MODEL, 6 * D_MODEL), 0.5 * D_MODEL ** -0.5),
        'b_ada': nrm(ks[13], (DEPTH, 6 * D_MODEL), 0.01),
        'g_pre_mix': gain(ks[14]),
        'g_post_mix': gain(ks[15]),
        'g_pre_ffn': gain(ks[16]),
        'g_post_ffn': gain(ks[17]),
        'w_in': nrm(ks[18], (DEPTH, D_MODEL, D_IN), D_MODEL ** -0.5),
        'w_pool_map': nrm(ks[19], (DEPTH, N_POOL_GROUPS, POOL_GROUP, POOL_GROUP), POOL_GROUP ** -0.5),
        'pool_scale': 1.0 + nrm(ks[20], (DEPTH, D_POOL), 0.1),
        'w_proj_pool': nrm(ks[21], (DEPTH, D_POOL, D_MODEL), D_POOL ** -0.5),
        'w_proj_att': nrm(ks[22], (DEPTH, D_ATT_OUT, D_MODEL), D_ATT_OUT ** -0.5),
        'w_out': nrm(ks[23], (DEPTH, D_MODEL, D_MODEL), D_MODEL ** -0.5),
        'w_up': nrm(ks[24], (DEPTH, D_MODEL, 2 * D_FF), D_MODEL ** -0.5),
        'w_conv': nrm(ks[25], (DEPTH, CONV_W, 2 * D_FF), CONV_W ** -0.5),
        'b_conv': nrm(ks[26], (DEPTH, 2 * D_FF), 0.01),
        'w_down': nrm(ks[27], (DEPTH, D_FF, D_MODEL), D_FF ** -0.5),
    }


def reference(x_prompt, x_sample, c_prompt, c_sample, cache_k_w128, cache_v_w128, cache_k_w512, cache_v_w512,
              cache_k_w2048, cache_v_w2048, state_pool, state_conv, w_ada, b_ada, g_pre_mix, g_post_mix,
              g_pre_ffn, g_post_ffn, w_in, w_pool_map, pool_scale, w_proj_pool, w_proj_att, w_out, w_up,
              w_conv, b_conv, w_down):
    params = (w_ada, b_ada, g_pre_mix, g_post_mix, g_pre_ffn, g_post_ffn, w_in, w_pool_map, pool_scale,
              w_proj_pool, w_proj_att, w_out, w_up, w_conv, b_conv, w_down)
    caches = ((cache_k_w128, cache_v_w128), (cache_k_w512, cache_v_w512), (cache_k_w2048, cache_v_w2048))
    bp, tp = x_prompt.shape[0], x_prompt.shape[1]
    pos_p = jnp.arange(tp)
    pos_s = PAST_LEN + jnp.arange(x_sample.shape[1])
    zeros_pool = jnp.zeros((bp, POOL_HIST, D_POOL), x_prompt.dtype)
    zeros_conv = jnp.zeros((bp, CONV_W - 1, 2 * D_FF), x_prompt.dtype)
    xp, xs = x_prompt, x_sample
    st_p, st_s = [], []
    for l in range(DEPTH):
        prm = tuple(w[l] for w in params)
        xp, sp = hybrid_layer(xp, c_prompt, pos_p, zeros_pool, zeros_conv, None, prm, True)
        kv_l = tuple((kc[l], vc[l]) for kc, vc in caches)
        xs, ss = hybrid_layer(xs, c_sample, pos_s, state_pool[l], state_conv[l], kv_l, prm, False)
        st_p.append(sp)
        st_s.append(ss)
    new_p = [jnp.stack([s[i] for s in st_p], axis=0) for i in range(8)]
    new_s = [jnp.stack([s[i] for s in st_s], axis=0) for i in range(8)]
    return (xp, xs, new_p[0], new_s[0], new_p[1], new_s[1], new_p[2], new_s[2], new_p[3], new_s[3],
            new_p[4], new_s[4], new_p[5], new_s[5], new_p[6], new_s[6], new_p[7], new_s[7])
```

```python
import functools

import jax
import jax.numpy as jnp
from jax import lax
from jax.experimental import pallas as pl
from jax.experimental.pallas import tpu as pltpu

F32 = jnp.float32
BF16 = jnp.bfloat16

D_MODEL = 1024
EPS = 1e-6
POOL_WINDOWS = (2, 4, 8, 16)
POOL_GROUP = D_MODEL // len(POOL_WINDOWS)
POOL_HIST = max(POOL_WINDOWS) - 1
ATT_GROUPS = ((128, 1), (512, 4), (2048, 16))
SPAN = 128
N_SLOTS = 8
HEAD_DIM = 64
D_GROUP = N_SLOTS * HEAD_DIM
D_ATT = len(ATT_GROUPS) * D_GROUP
D_FF = 2816
CONV_W = 3
COL_U, COL_Q, COL_K, COL_V, COL_ZP, COL_ZA, D_IN = 0, 1024, 2560, 4096, 5632, 6656, 7680
Q_SCALE = HEAD_DIM ** -0.5
NEG = -1e30

LANES = 128
HALO = 16
CONV_HALO = 8
TM = 256
FF_CHUNK = 1408
VMEM_LIMIT = 56 * 1024 * 1024


def _const_spec(shape):
    nd = len(shape)
    return pl.BlockSpec(shape, lambda *_: (0,) * nd, pipeline_mode=pl.Buffered(1))


def _rms(x, g):
    return x * lax.rsqrt(jnp.mean(x * x, axis=-1, keepdims=True) + EPS) * g


def _sigmoid(x):
    return 1.0 / (1.0 + jnp.exp(-x))


def _gelu_tanh(x):
    return 0.5 * x * (1.0 + jnp.tanh(0.7978845608028654 * (x + 0.044715 * (x * x * x))))


def _dot(a, b):
    return jnp.dot(a, b, preferred_element_type=F32)


def _ada_kernel(c_ref, w_ref, b_ref, o_ref):
    c = c_ref[...]
    s = c * _sigmoid(c)
    o_ref[...] = _dot(s.astype(BF16), w_ref[...].astype(BF16)) + b_ref[...]


def _ada(c_all, w_ada, b_ada):
    m = c_all.shape[0]
    n = w_ada.shape[1]
    tn = 1536
    return pl.pallas_call(
        _ada_kernel,
        grid=(n // tn,),
        in_specs=[pl.BlockSpec((m, D_MODEL), lambda j: (0, 0)),
                  pl.BlockSpec((D_MODEL, tn), lambda j: (0, j)),
                  pl.BlockSpec((1, tn), lambda j: (0, j))],
        out_specs=pl.BlockSpec((m, tn), lambda j: (0, j)),
        out_shape=jax.ShapeDtypeStruct((m, n), F32),
        compiler_params=pltpu.CompilerParams(dimension_semantics=("arbitrary",),
                                             vmem_limit_bytes=VMEM_LIMIT),
        name="ada",
    )(c_all, w_ada, b_ada)


def _front_kernel(x_ref, mod_ref, g_ref, win_ref, wmap_ref, pscale_ref, wpp_ref,
                  gp_ref, sg_ref, q0, k0, v0, q1, k1, v1, q2, k2, v2,
                  kf0, vf0, kf1, vf1, kf2, vf2, utail_ref, ubuf):
    i = pl.program_id(1)
    tm = x_ref.shape[1]
    x = x_ref[0]
    shift = mod_ref[0, 0:1, :]
    scale = mod_ref[0, 1:2, :]
    h = (_rms(x, g_ref[...]) * (1.0 + scale) + shift).astype(BF16)

    u = _dot(h, win_ref[:, COL_U:COL_Q])

    @pl.when(i == 0)
    def _():
        ubuf[0:HALO, :] = jnp.zeros((HALO, D_MODEL), F32)

    @pl.when(i > 0)
    def _():
        ubuf[0:HALO, :] = ubuf[tm:tm + HALO, :]

    ubuf[HALO:HALO + tm, :] = u
    utail_ref[0] = u[tm - HALO:, :]

    pos = i * tm + lax.broadcasted_iota(jnp.int32, (tm, 1), 0)
    mixed = []
    for g, win in enumerate(POOL_WINDOWS):
        lo, hi = g * POOL_GROUP, (g + 1) * POOL_GROUP
        wsum = ubuf[HALO:HALO + tm, lo:hi]
        for j in range(1, win):
            wsum = wsum + ubuf[HALO - j:HALO - j + tm, lo:hi]
        count = jnp.minimum(pos + 1, win).astype(F32)
        pooled = wsum / count - ubuf[HALO:HALO + tm, lo:hi]
        mixed.append(_dot(pooled.astype(BF16), wmap_ref[g]))
    mixed = jnp.concatenate(mixed, axis=-1) * pscale_ref[...]
    y_pool = _dot(mixed.astype(BF16), wpp_ref[...])
    gp_ref[0] = (_sigmoid(_dot(h, win_ref[:, COL_ZP:COL_ZA])) * y_pool).astype(BF16)
    sg_ref[0] = _sigmoid(_dot(h, win_ref[:, COL_ZA:D_IN])).astype(BF16)

    q = _dot(h, win_ref[:, COL_Q:COL_K]) * Q_SCALE
    for g, ref in enumerate((q0, q1, q2)):
        ref[0] = q[:, g * D_GROUP:(g + 1) * D_GROUP].astype(BF16)
    k = _dot(h, win_ref[:, COL_K:COL_V])
    for g, (ref, fref) in enumerate(((k0, kf0), (k1, kf1), (k2, kf2))):
        kg = k[:, g * D_GROUP:(g + 1) * D_GROUP]
        ref[0] = kg.astype(BF16)
        fref[0] = kg[tm - fref.shape[1]:, :]
    v = _dot(h, win_ref[:, COL_V:COL_ZP])
    for g, (ref, fref) in enumerate(((v0, vf0), (v1, vf1), (v2, vf2))):
        vg = v[:, g * D_GROUP:(g + 1) * D_GROUP]
        ref[0] = vg.astype(BF16)
        fref[0] = vg[tm - fref.shape[1]:, :]


def _front(x, mod, g_pre, w_in, w_map, pool_scale, w_pp):
    b, t, d = x.shape
    tm = TM
    nblk = t // tm
    row = lambda shape: pl.BlockSpec(shape, lambda bi, i: (bi, i, 0))
    in_specs = [row((1, tm, d)),
                pl.BlockSpec((1, 6, d), lambda bi, i: (bi, 0, 0)),
                _const_spec((1, d)), _const_spec(w_in.shape), _const_spec(w_map.shape),
                _const_spec((1, d)), _const_spec(w_pp.shape)]
    out_shape = [jax.ShapeDtypeStruct((b, t, d), BF16), jax.ShapeDtypeStruct((b, t, d), BF16)]
    out_specs = [row((1, tm, d)), row((1, tm, d))]
    for _ in range(9):
        out_shape.append(jax.ShapeDtypeStruct((b, t, D_GROUP), BF16))
        out_specs.append(row((1, tm, D_GROUP)))
    for win, _ in ATT_GROUPS:
        keep = min(win, t)
        rows = min(keep, tm)
        first = nblk - keep // rows if keep >= tm else 0
        for _ in range(2):
            out_shape.append(jax.ShapeDtypeStruct((b, keep, D_GROUP), F32))
            if keep >= tm:
                out_specs.append(pl.BlockSpec(
                    (1, rows, D_GROUP), lambda bi, i, first=first: (bi, jnp.maximum(i - first, 0), 0)))
            else:
                out_specs.append(pl.BlockSpec((1, rows, D_GROUP), lambda bi, i: (bi, 0, 0)))
    out_shape.append(jax.ShapeDtypeStruct((b, HALO, d), F32))
    out_specs.append(pl.BlockSpec((1, HALO, d), lambda bi, i: (bi, 0, 0)))
    return pl.pallas_call(
        _front_kernel,
        grid=(b, nblk),
        in_specs=in_specs, out_specs=out_specs, out_shape=out_shape,
        scratch_shapes=[pltpu.VMEM((HALO + tm, d), F32)],
        compiler_params=pltpu.CompilerParams(dimension_semantics=("arbitrary", "arbitrary"),
                                             vmem_limit_bytes=VMEM_LIMIT),
        name="front",
    )(x, mod, g_pre, w_in, w_map, pool_scale, w_pp)


def _attn_kernel(q_ref, kp_ref, kc_ref, vp_ref, vc_ref, o_ref, lse_ref):
    n = pl.program_id(1)
    s2 = 2 * SPAN
    row = lax.broadcasted_iota(jnp.int32, (s2, s2), 0) % SPAN
    col = lax.broadcasted_iota(jnp.int32, (s2, s2), 1)
    first_row = row + jnp.where(n > 0, 0, s2)
    mask = jnp.where(col < SPAN, col - first_row, row - (col - SPAN)) >= 0
    lane = lax.broadcasted_iota(jnp.int32, (SPAN, LANES), 1)
    low = lane < HEAD_DIM
    lse_tile = jnp.zeros((SPAN, LANES), F32)
    for j in range(N_SLOTS // 2):
        sl = slice(j * LANES, (j + 1) * LANES)
        q2 = q_ref[0, :, sl]
        zero = jnp.zeros_like(q2)
        qs = jnp.concatenate([jnp.where(low, q2, zero), jnp.where(low, zero, q2)], axis=0)
        kk = jnp.concatenate([kp_ref[0, :, sl], kc_ref[0, :, sl]], axis=0)
        vv = jnp.concatenate([vp_ref[0, :, sl], vc_ref[0, :, sl]], axis=0)
        s = lax.dot_general(qs, kk, (((1,), (1,)), ((), ())), preferred_element_type=F32)
        s = jnp.where(mask, s, NEG)
        m = jnp.max(s, axis=-1, keepdims=True)
        p = jnp.exp(s - m)
        den = jnp.sum(p, axis=-1, keepdims=True)
        o2 = _dot(p.astype(BF16), vv) / den
        lse2 = m + jnp.log(den)
        o_ref[0, :, sl] = jnp.where(low, o2[:SPAN], o2[SPAN:])
        lse_tile = jnp.where(lane == 2 * j, lse2[:SPAN], lse_tile)
        lse_tile = jnp.where(lane == 2 * j + 1, lse2[SPAN:], lse_tile)
    lse_ref[0] = lse_tile


def _attn(q, k, v):
    bd, n_sub, _ = q.shape
    nb = n_sub // SPAN
    cur = pl.BlockSpec((1, SPAN, D_GROUP), lambda b, n: (b, n, 0))
    prev = pl.BlockSpec((1, SPAN, D_GROUP), lambda b, n: (b, jnp.maximum(n - 1, 0), 0))
    return pl.pallas_call(
        _attn_kernel,
        grid=(bd, nb),
        in_specs=[cur, prev, cur, prev, cur],
        out_specs=[cur, pl.BlockSpec((1, SPAN, LANES), lambda b, n: (b, n, 0))],
        out_shape=[jax.ShapeDtypeStruct((bd, n_sub, D_GROUP), F32),
                   jax.ShapeDtypeStruct((bd, n_sub, LANES), F32)],
        compiler_params=pltpu.CompilerParams(dimension_semantics=("arbitrary", "arbitrary"),
                                             vmem_limit_bytes=VMEM_LIMIT),
        name="attn",
    )(q, k, k, v, v)


def _merge_groups(o_list, lse_list, expand):
    top = jnp.maximum(jnp.maximum(lse_list[0], lse_list[1]), lse_list[2])
    e = [jnp.exp(l - top) for l in lse_list]
    den = e[0] + e[1] + e[2]
    out = None
    for eg, og in zip(e, o_list):
        w = eg / den
        hi = w.astype(BF16)
        lo = (w - hi.astype(F32)).astype(BF16)
        term = (_dot(hi, expand) + _dot(lo, expand)) * og
        out = term if out is None else out + term
    return out


def _expand_matrix():
    r = lax.broadcasted_iota(jnp.int32, (LANES, D_GROUP), 0)
    c = lax.broadcasted_iota(jnp.int32, (LANES, D_GROUP), 1)
    return jnp.where(c // HEAD_DIM == r, 1.0, 0.0).astype(BF16)


def _ffn_tail(h2, hbuf, rows, wup_ref, wconv_ref, bconv_ref, wdown_ref):
    for c in range(2 * D_FF // FF_CHUNK):
        cs = slice(c * FF_CHUNK, (c + 1) * FF_CHUNK)
        hbuf[CONV_HALO:CONV_HALO + rows, cs] = _dot(h2, wup_ref[:, cs])
    acc = jnp.zeros((rows, D_MODEL), F32)
    for c in range(D_FF // FF_CHUNK):
        halves = []
        for base in (0, D_FF):
            cs = slice(base + c * FF_CHUNK, base + (c + 1) * FF_CHUNK)
            hc = bconv_ref[:, cs]
            for j in range(CONV_W):
                off = CONV_HALO - (CONV_W - 1) + j
                hc = hc + hbuf[off:off + rows, cs] * wconv_ref[j:j + 1, cs]
            halves.append(hc)
        gated = (_gelu_tanh(halves[0]) * halves[1]).astype(BF16)
        acc = acc + _dot(gated, wdown_ref[c * FF_CHUNK:(c + 1) * FF_CHUNK, :])
    return acc


def _back_kernel(x_ref, mod_ref, gp_ref, sg_ref, o0, o1, o2, l0, l1, l2,
                 wpa_ref, wout_ref, wup_ref, wconv_ref, bconv_ref, wdown_ref,
                 gpost_ref, gpre_ref, gffn_ref, y_ref, ctail_ref, hbuf):
    i = pl.program_id(1)
    tm = x_ref.shape[1]
    mod = lambda r: mod_ref[0, r:r + 1, :]
    o = _merge_groups([o0[0], o1[0], o2[0]], [l0[0], l1[0], l2[0]], _expand_matrix())
    y_att = _dot(o.astype(BF16), wpa_ref[...])
    mix = gp_ref[0].astype(F32) + sg_ref[0].astype(F32) * y_att
    x1 = x_ref[0] + mod(2) * _rms(_dot(mix.astype(BF16), wout_ref[...]), gpost_ref[...])
    h2 = (_rms(x1, gpre_ref[...]) * (1.0 + mod(4)) + mod(3)).astype(BF16)

    @pl.when(i == 0)
    def _():
        hbuf[0:CONV_HALO, :] = jnp.zeros((CONV_HALO, 2 * D_FF), F32)

    @pl.when(i > 0)
    def _():
        hbuf[0:CONV_HALO, :] = hbuf[tm:tm + CONV_HALO, :]

    ffn = _ffn_tail(h2, hbuf, tm, wup_ref, wconv_ref, bconv_ref, wdown_ref)
    y_ref[0] = x1 + mod(5) * _rms(ffn, gffn_ref[...])
    ctail_ref[0] = hbuf[tm:tm + CONV_HALO, :]


def _back(x, mod, gp, sg, o_list, lse_list, w_pa, w_out, w_up, w_conv, b_conv, w_down,
          g_post, g_pre, g_ffn):
    b, t, d = x.shape
    tm = TM
    row = lambda shape: pl.BlockSpec(shape, lambda bi, i: (bi, i, 0))
    in_specs = ([row((1, tm, d)), pl.BlockSpec((1, 6, d), lambda bi, i: (bi, 0, 0)),
                 row((1, tm, d)), row((1, tm, d))]
                + [row((1, tm, D_GROUP))] * 3 + [row((1, tm, LANES))] * 3
                + [_const_spec(w.shape) for w in (w_pa, w_out, w_up, w_conv, b_conv, w_down)]
                + [_const_spec((1, d))] * 3)
    return pl.pallas_call(
        _back_kernel,
        grid=(b, t // tm),
        in_specs=in_specs,
        out_specs=[row((1, tm, d)), pl.BlockSpec((1, CONV_HALO, 2 * D_FF), lambda bi, i: (bi, 0, 0))],
        out_shape=[jax.ShapeDtypeStruct((b, t, d), F32),
                   jax.ShapeDtypeStruct((b, CONV_HALO, 2 * D_FF), F32)],
        scratch_shapes=[pltpu.VMEM((CONV_HALO + tm, 2 * D_FF), F32)],
        compiler_params=pltpu.CompilerParams(dimension_semantics=("arbitrary", "arbitrary"),
                                             vmem_limit_bytes=VMEM_LIMIT),
        name="back",
    )(x, mod, gp, sg, *o_list, *lse_list, w_pa, w_out, w_up, w_conv, b_conv, w_down,
      g_post, g_pre, g_ffn)


def _sfront_kernel(x_ref, shift_ref, scale_ref, g_ref, w_ref, o_ref, h_scr):
    @pl.when(pl.program_id(0) == 0)
    def _():
        h = _rms(x_ref[...], g_ref[...]) * (1.0 + scale_ref[...]) + shift_ref[...]
        h_scr[...] = h.astype(BF16)

    o_ref[...] = _dot(h_scr[...], w_ref[...])


def _sfront(x, mod, g_pre, w_in):
    m, d = x.shape
    n = w_in.shape[1]
    tn = 1536
    return pl.pallas_call(
        _sfront_kernel,
        grid=(n // tn,),
        in_specs=[pl.BlockSpec((m, d), lambda j: (0, 0)),
                  pl.BlockSpec((m, d), lambda j: (0, 0)),
                  pl.BlockSpec((m, d), lambda j: (0, 1)),
                  pl.BlockSpec((1, d), lambda j: (0, 0)),
                  pl.BlockSpec((d, tn), lambda j: (0, j))],
        out_specs=pl.BlockSpec((m, tn), lambda j: (0, j)),
        out_shape=jax.ShapeDtypeStruct((m, n), F32),
        scratch_shapes=[pltpu.VMEM((m, d), BF16)],
        compiler_params=pltpu.CompilerParams(dimension_semantics=("arbitrary",),
                                             vmem_limit_bytes=VMEM_LIMIT),
        name="sfront",
    )(x, mod, mod, g_pre, w_in)


def _sattn_kernel(q_ref, kn_ref, vn_ref, kc0, vc0, kc1, vc1, kc2, vc2, o_ref):
    outs, lses = [], []
    for g, (kc, vc) in enumerate(((kc0, vc0), (kc1, vc1), (kc2, vc2))):
        q = q_ref[:, g:g + 1] * Q_SCALE
        kn, vn = kn_ref[:, g:g + 1], vn_ref[:, g:g + 1]
        s = jnp.sum(kc[...] * q, axis=-1, keepdims=True)
        s_new = jnp.sum(kn * q, axis=-1, keepdims=True)
        m = jnp.maximum(jnp.max(s, axis=1, keepdims=True), s_new)
        p = jnp.exp(s - m)
        p_new = jnp.exp(s_new - m)
        den = jnp.sum(p, axis=1, keepdims=True) + p_new
        outs.append((jnp.sum(p * vc[...], axis=1, keepdims=True) + p_new * vn) / den)
        lses.append(m + jnp.log(den))
    top = jnp.maximum(jnp.maximum(lses[0], lses[1]), lses[2])
    e = [jnp.exp(l - top) for l in lses]
    den = e[0] + e[1] + e[2]
    o_ref[...] = (outs[0] * e[0] + outs[1] * e[1] + outs[2] * e[2]) / den


def _sattn(q, kn, vn, caches):
    b = q.shape[0]
    bb = 4
    new = pl.BlockSpec((bb, len(ATT_GROUPS), N_SLOTS, HEAD_DIM), lambda i: (i, 0, 0, 0))
    in_specs = [new, new, new]
    args = [q, kn, vn]
    for (win, dil), (kc, vc) in zip(ATT_GROUPS, caches):
        for c in (kc, vc):
            args.append(c.reshape(b, win // dil, dil, N_SLOTS, HEAD_DIM))
            in_specs.append(pl.BlockSpec((bb, win // dil, None, N_SLOTS, HEAD_DIM),
                                         lambda i: (i, 0, 0, 0, 0)))
    return pl.pallas_call(
        _sattn_kernel,
        grid=(b // bb,),
        in_specs=in_specs,
        out_specs=pl.BlockSpec((bb, 1, N_SLOTS, HEAD_DIM), lambda i: (i, 0, 0, 0)),
        out_shape=jax.ShapeDtypeStruct((b, 1, N_SLOTS, HEAD_DIM), F32),
        compiler_params=pltpu.CompilerParams(dimension_semantics=("arbitrary",),
                                             vmem_limit_bytes=VMEM_LIMIT),
        name="sattn",
    )(*args)


def _sback_kernel(x_ref, mod_ref, proj_ref, hist_ref, o_ref, chist_ref,
                  wmap_ref, pscale_ref, wpp_ref, wpa_ref, wout_ref, wup_ref, wconv_ref, bconv_ref,
                  wdown_ref, gpost_ref, gpre_ref, gffn_ref, y_ref, hu_ref, hbuf):
    m = x_ref.shape[0]
    mod = lambda r: mod_ref[:, r * D_MODEL:(r + 1) * D_MODEL]
    u = proj_ref[:, COL_U:COL_Q]
    mixed = []
    for g, win in enumerate(POOL_WINDOWS):
        lo, hi = g * POOL_GROUP, (g + 1) * POOL_GROUP
        wsum = u[:, lo:hi]
        for j in range(1, win):
            wsum = wsum + hist_ref[POOL_HIST - j, :, lo:hi]
        pooled = wsum / float(win) - u[:, lo:hi]
        mixed.append(_dot(pooled.astype(BF16), wmap_ref[g]))
    mixed = jnp.concatenate(mixed, axis=-1) * pscale_ref[...]
    y_pool = _dot(mixed.astype(BF16), wpp_ref[...])
    y_att = _dot(o_ref[...].astype(BF16), wpa_ref[...])
    mix = (_sigmoid(proj_ref[:, COL_ZP:COL_ZA]) * y_pool + _sigmoid(proj_ref[:, COL_ZA:D_IN]) * y_att)
    x1 = x_ref[...] + mod(2) * _rms(_dot(mix.astype(BF16), wout_ref[...]), gpost_ref[...])
    h2 = (_rms(x1, gpre_ref[...]) * (1.0 + mod(4)) + mod(3)).astype(BF16)

    for c in range(2 * D_FF // FF_CHUNK):
        cs = slice(c * FF_CHUNK, (c + 1) * FF_CHUNK)
        hu_c = _dot(h2, wup_ref[:, cs])
        hu_ref[:, cs] = hu_c
        hbuf[:, cs] = hu_c
    acc = jnp.zeros((m, D_MODEL), F32)
    for c in range(D_FF // FF_CHUNK):
        halves = []
        for base in (0, D_FF):
            cs = slice(base + c * FF_CHUNK, base + (c + 1) * FF_CHUNK)
            hc = bconv_ref[:, cs]
            for j in range(CONV_W - 1):
                hc = hc + chist_ref[j, :, cs] * wconv_ref[j:j + 1, cs]
            hc = hc + hbuf[:, cs] * wconv_ref[CONV_W - 1:CONV_W, cs]
            halves.append(hc)
        gated = (_gelu_tanh(halves[0]) * halves[1]).astype(BF16)
        acc = acc + _dot(gated, wdown_ref[c * FF_CHUNK:(c + 1) * FF_CHUNK, :])
    y_ref[...] = x1 + mod(5) * _rms(acc, gffn_ref[...])


def _sback(x, mod, proj, hist_t, o, chist_t, weights, gains):
    m, d = x.shape
    args = [x, mod, proj, hist_t, o, chist_t, *weights, *gains]
    return pl.pallas_call(
        _sback_kernel,
        grid=(1,),
        in_specs=[_const_spec(a.shape) for a in args],
        out_specs=[pl.BlockSpec((m, d), lambda i: (0, 0)),
                   pl.BlockSpec((m, 2 * D_FF), lambda i: (0, 0))],
        out_shape=[jax.ShapeDtypeStruct((m, d), F32), jax.ShapeDtypeStruct((m, 2 * D_FF), F32)],
        scratch_shapes=[pltpu.VMEM((m, 2 * D_FF), F32)],
        compiler_params=pltpu.CompilerParams(dimension_semantics=("arbitrary",),
                                             vmem_limit_bytes=VMEM_LIMIT),
        name="sback",
    )(*args)


SHIFT_CHUNKS = 8


def _shift_copies(cache_refs, new_refs, out_refs, sem):
    copies = []
    for t, (c, nw, o) in enumerate(zip(cache_refs, new_refs, out_refs)):
        b, length = c.shape[0], c.shape[1]
        cb = b // SHIFT_CHUNKS
        for ch in range(SHIFT_CHUNKS):
            bs = pl.ds(ch * cb, cb)
            copies.append(pltpu.make_async_copy(
                c.at[bs, pl.ds(1, length - 1)], o.at[bs, pl.ds(0, length - 1)], sem.at[t, 0]))
            copies.append(pltpu.make_async_copy(
                nw.at[bs], o.at[bs, pl.ds(length - 1, 1)], sem.at[t, 1]))
    return copies


def _shift_kernel(*refs):
    n = (len(refs) - 1) // 3
    copies = _shift_copies(refs[:n], refs[n:2 * n], refs[2 * n:3 * n], refs[3 * n])
    for c in copies:
        c.start()
    for c in copies:
        c.wait()


def _shift(caches, news):
    n = len(caches)
    anyspec = pl.BlockSpec(memory_space=pl.ANY)
    return pl.pallas_call(
        _shift_kernel,
        in_specs=[anyspec] * (2 * n),
        out_specs=[anyspec] * n,
        out_shape=[jax.ShapeDtypeStruct(c.shape, c.dtype) for c in caches],
        scratch_shapes=[pltpu.SemaphoreType.DMA((n, 2))],
        name="cache_shift",
    )(*caches, *news)


def _to_residues(a, dil):
    if dil == 1:
        return a
    b, t, c = a.shape
    return a.reshape(b, t // dil, dil, c).transpose(0, 2, 1, 3).reshape(b * dil, t // dil, c)


def _from_residues(a, dil, b):
    if dil == 1:
        return a
    bd, n_sub, c = a.shape
    return a.reshape(b, dil, n_sub, c).transpose(0, 2, 1, 3).reshape(b, n_sub * dil, c)


def kernel(x_prompt, x_sample, c_prompt, c_sample, cache_k_w128, cache_v_w128, cache_k_w512, cache_v_w512,
           cache_k_w2048, cache_v_w2048, state_pool, state_conv, w_ada, b_ada, g_pre_mix, g_post_mix,
           g_pre_ffn, g_post_ffn, w_in, w_pool_map, pool_scale, w_proj_pool, w_proj_att, w_out, w_up,
           w_conv, b_conv, w_down):
    bp, tp, d = x_prompt.shape
    bs = x_sample.shape[0]
    (w_ada, b_ada, g_pre_mix, g_post_mix, g_pre_ffn, g_post_ffn, w_in, w_pool_map, pool_scale,
     w_proj_pool, w_proj_att, w_out, w_up, w_conv, b_conv, w_down) = (
        w[0] for w in (w_ada, b_ada, g_pre_mix, g_post_mix, g_pre_ffn, g_post_ffn, w_in, w_pool_map,
                       pool_scale, w_proj_pool, w_proj_att, w_out, w_up, w_conv, b_conv, w_down))
    caches = [(cache_k_w128[0], cache_v_w128[0]), (cache_k_w512[0], cache_v_w512[0]),
              (cache_k_w2048[0], cache_v_w2048[0])]
    g_pre_mix, g_post_mix, g_pre_ffn, g_post_ffn, pool_scale, b_conv, b_ada = (
        a.reshape(1, -1) for a in (g_pre_mix, g_post_mix, g_pre_ffn, g_post_ffn, pool_scale, b_conv, b_ada))
    w_in_b, w_map_b, w_pp_b, w_pa_b, w_out_b, w_up_b, w_down_b = (
        w.astype(BF16) for w in (w_in, w_pool_map, w_proj_pool, w_proj_att, w_out, w_up, w_down))

    mod = _ada(jnp.concatenate([c_prompt, c_sample], axis=0), w_ada, b_ada)
    mod_p = mod[:bp].reshape(bp, 6, d)
    mod_s = mod[bp:]

    proj_s = _sfront(x_sample[:, 0], mod_s, g_pre_mix, w_in_b)
    heads = lambda lo: proj_s[:, lo:lo + D_ATT].reshape(bs, len(ATT_GROUPS), N_SLOTS, HEAD_DIM)
    q_s, k_s, v_s = heads(COL_Q), heads(COL_K), heads(COL_V)
    o_s = _sattn(q_s, k_s, v_s, caches).reshape(bs, D_GROUP)
    hist_t = jnp.transpose(state_pool[0], (1, 0, 2))
    chist_t = jnp.transpose(state_conv[0], (1, 0, 2))
    y_s, hu_s = _sback(x_sample[:, 0], mod_s, proj_s, hist_t, o_s, chist_t,
                       (w_map_b, pool_scale, w_pp_b, w_pa_b, w_out_b, w_up_b, w_conv, b_conv, w_down_b),
                       (g_post_mix, g_pre_ffn, g_post_ffn))
    flat_caches = [c for pair in caches for c in pair]
    news = []
    for g in range(len(ATT_GROUPS)):
        news += [k_s[:, g:g + 1], v_s[:, g:g + 1]]
    new_caches = _shift(flat_caches, news)
    pool_state_s = jnp.concatenate([state_pool[0][:, 1:], proj_s[:, None, COL_U:COL_Q]], axis=1)
    conv_state_s = jnp.concatenate([state_conv[0][:, 1:], hu_s[:, None]], axis=1)

    front = _front(x_prompt, mod_p, g_pre_mix, w_in_b, w_map_b, pool_scale, w_pp_b)
    gp, sg = front[0], front[1]
    qkv = front[2:11]
    kv_f32 = front[11:17]
    u_tail = front[17]
    o_list, lse_list = [], []
    for g, (_, dil) in enumerate(ATT_GROUPS):
        q_g, k_g, v_g = (_to_residues(qkv[3 * g + j], dil) for j in range(3))
        o_g, lse_g = _attn(q_g, k_g, v_g)
        o_list.append(_from_residues(o_g, dil, bp))
        lse_list.append(_from_residues(lse_g, dil, bp))
    y_p, conv_tail = _back(x_prompt, mod_p, gp, sg, o_list, lse_list, w_pa_b, w_out_b, w_up_b, w_conv,
                           b_conv, w_down_b, g_post_mix, g_pre_ffn, g_post_ffn)

    outs = [y_p, y_s[:, None]]
    for g in range(len(ATT_GROUPS)):
        for j in range(2):
            f = kv_f32[2 * g + j]
            outs.append(f.reshape(1, bp, f.shape[1], N_SLOTS, HEAD_DIM))
            outs.append(new_caches[2 * g + j][None])
    outs += [u_tail[None, :, HALO - POOL_HIST:], pool_state_s[None],
             conv_tail[None, :, CONV_HALO - (CONV_W - 1):], conv_state_s[None]]
    return tuple(outs)
```

```python
import functools

import jax
import jax.numpy as jnp
from jax import lax
from jax.experimental import pallas as pl
from jax.experimental.pallas import tpu as pltpu

F32 = jnp.float32
BF16 = jnp.bfloat16

D_MODEL = 1024
EPS = 1e-6
POOL_WINDOWS = (2, 4, 8, 16)
POOL_GROUP = D_MODEL // len(POOL_WINDOWS)
POOL_HIST = max(POOL_WINDOWS) - 1
ATT_GROUPS = ((128, 1), (512, 4), (2048, 16))
SPAN = 128
N_GROUPS = len(ATT_GROUPS)
N_SLOTS = 8
HEAD_DIM = 64
D_GROUP = N_SLOTS * HEAD_DIM
D_ATT = N_GROUPS * D_GROUP
D_FF = 2816
CONV_W = 3
COL_U, COL_Q, COL_K, COL_V, COL_ZP, COL_ZA, D_IN = 0, 1024, 2560, 4096, 5632, 6656, 7680
Q_SCALE = HEAD_DIM ** -0.5
NEG = -1e30

LANES = 128
GROUP_SLABS = D_GROUP // LANES
HALO = 16
CONV_HALO = 8
TM = 256
FF_CHUNK = 1408
VMEM_LIMIT = 56 * 1024 * 1024
CACHE_BLOCK_BYTES = 4 * 1024 * 1024


def _const_spec(shape):
    nd = len(shape)
    return pl.BlockSpec(shape, lambda *_: (0,) * nd, pipeline_mode=pl.Buffered(1))


def _rms(x, g):
    return x * lax.rsqrt(jnp.mean(x * x, axis=-1, keepdims=True) + EPS) * g


def _sigmoid(x):
    return 1.0 / (1.0 + jnp.exp(-x))


def _gelu_tanh(x):
    return 0.5 * x * (1.0 + jnp.tanh(0.7978845608028654 * (x + 0.044715 * (x * x * x))))


def _dot(a, b):
    return jnp.dot(a, b, preferred_element_type=F32)


def _params(*semantics):
    return pltpu.CompilerParams(dimension_semantics=semantics, vmem_limit_bytes=VMEM_LIMIT)


def _ada_kernel(c_ref, w_ref, b_ref, o_ref):
    c = c_ref[...]
    s = c * _sigmoid(c)
    o_ref[...] = _dot(s.astype(BF16), w_ref[...].astype(BF16)) + b_ref[...]


def _ada(c_all, w_ada, b_ada):
    m = c_all.shape[0]
    n = w_ada.shape[1]
    tn = 1536
    return pl.pallas_call(
        _ada_kernel,
        grid=(n // tn,),
        in_specs=[pl.BlockSpec((m, D_MODEL), lambda j: (0, 0)),
                  pl.BlockSpec((D_MODEL, tn), lambda j: (0, j)),
                  pl.BlockSpec((1, tn), lambda j: (0, j))],
        out_specs=pl.BlockSpec((m, tn), lambda j: (0, j)),
        out_shape=jax.ShapeDtypeStruct((m, n), F32),
        compiler_params=_params("arbitrary"),
        name="ada",
    )(c_all, w_ada, b_ada)


def _front_kernel(x_ref, mod_ref, g_ref, win_ref, wmap_ref, pscale_ref, wpp_ref,
                  gp_ref, sg_ref, q0, k0, v0, q1, k1, v1, q2, k2, v2,
                  kf0, vf0, kf1, vf1, kf2, vf2, utail_ref, ubuf, slabs):
    i = pl.program_id(1)
    tm = x_ref.shape[1]
    x = x_ref[0]
    shift = mod_ref[0, 0:1, :]
    scale = mod_ref[0, 1:2, :]
    h = (_rms(x, g_ref[...]) * (1.0 + scale) + shift).astype(BF16)

    u = _dot(h, win_ref[:, COL_U:COL_Q])

    @pl.when(i == 0)
    def _():
        ubuf[0:HALO, :] = jnp.zeros((HALO, D_MODEL), F32)

    @pl.when(i > 0)
    def _():
        ubuf[0:HALO, :] = ubuf[tm:tm + HALO, :]

    ubuf[HALO:HALO + tm, :] = u
    utail_ref[0] = u[tm - HALO:, :]

    pos = i * tm + lax.broadcasted_iota(jnp.int32, (tm, 1), 0)
    mixed = []
    for g, win in enumerate(POOL_WINDOWS):
        lo, hi = g * POOL_GROUP, (g + 1) * POOL_GROUP
        wsum = ubuf[HALO:HALO + tm, lo:hi]
        for j in range(1, win):
            wsum = wsum + ubuf[HALO - j:HALO - j + tm, lo:hi]
        count = jnp.minimum(pos + 1, win).astype(F32)
        pooled = wsum / count - ubuf[HALO:HALO + tm, lo:hi]
        mixed.append(_dot(pooled.astype(BF16), wmap_ref[g]))
    mixed = jnp.concatenate(mixed, axis=-1) * pscale_ref[...]
    y_pool = _dot(mixed.astype(BF16), wpp_ref[...])
    gp_ref[0] = (_sigmoid(_dot(h, win_ref[:, COL_ZP:COL_ZA])) * y_pool).astype(BF16)
    sg_ref[0] = _sigmoid(_dot(h, win_ref[:, COL_ZA:D_IN])).astype(BF16)

    def emit(val, refs, frefs):
        for j in range(D_ATT // LANES):
            slabs[j] = val[:, j * LANES:(j + 1) * LANES]
        for g, (_, dil) in enumerate(ATT_GROUPS):
            if dil == 1:
                refs[g][0, 0] = val[:, g * D_GROUP:(g + 1) * D_GROUP].astype(BF16)
                continue
            for r in range(dil):
                for jj in range(GROUP_SLABS):
                    piece = slabs[g * GROUP_SLABS + jj, pl.ds(r, tm // dil, stride=dil), :]
                    refs[g][0, r, :, jj * LANES:(jj + 1) * LANES] = piece.astype(BF16)
        for g, fref in enumerate(frefs):
            rows = fref.shape[2]
            fref[0] = val[tm - rows:, g * D_GROUP:(g + 1) * D_GROUP].T

    emit(_dot(h, win_ref[:, COL_Q:COL_K]) * Q_SCALE, (q0, q1, q2), ())
    emit(_dot(h, win_ref[:, COL_K:COL_V]), (k0, k1, k2), (kf0, kf1, kf2))
    emit(_dot(h, win_ref[:, COL_V:COL_ZP]), (v0, v1, v2), (vf0, vf1, vf2))


def _front(x, mod, g_pre, w_in, w_map, pool_scale, w_pp):
    b, t, d = x.shape
    tm = TM
    nblk = t // tm
    row = lambda shape: pl.BlockSpec(shape, lambda bi, i: (bi, i, 0))
    in_specs = [row((1, tm, d)),
                pl.BlockSpec((1, 6, d), lambda bi, i: (bi, 0, 0)),
                _const_spec((1, d)), _const_spec(w_in.shape), _const_spec(w_map.shape),
                _const_spec((1, d)), _const_spec(w_pp.shape)]
    out_shape = [jax.ShapeDtypeStruct((b, t, d), BF16), jax.ShapeDtypeStruct((b, t, d), BF16)]
    out_specs = [row((1, tm, d)), row((1, tm, d))]
    for _, dil in ATT_GROUPS:
        for _ in range(3):
            out_shape.append(jax.ShapeDtypeStruct((b, dil, t // dil, D_GROUP), BF16))
            out_specs.append(pl.BlockSpec((1, dil, tm // dil, D_GROUP), lambda bi, i: (bi, 0, i, 0)))
    for win, _ in ATT_GROUPS:
        keep = min(win, t)
        rows = min(keep, tm)
        first = nblk - keep // rows
        for _ in range(2):
            out_shape.append(jax.ShapeDtypeStruct((b, D_GROUP, keep), F32))
            out_specs.append(pl.BlockSpec(
                (1, D_GROUP, rows), lambda bi, i, first=first: (bi, 0, jnp.maximum(i - first, 0))))
    out_shape.append(jax.ShapeDtypeStruct((b, HALO, d), F32))
    out_specs.append(pl.BlockSpec((1, HALO, d), lambda bi, i: (bi, 0, 0)))
    return pl.pallas_call(
        _front_kernel,
        grid=(b, nblk),
        in_specs=in_specs, out_specs=out_specs, out_shape=out_shape,
        scratch_shapes=[pltpu.VMEM((HALO + tm, d), F32),
                        pltpu.VMEM((D_ATT // LANES, tm, LANES), F32)],
        compiler_params=_params("arbitrary", "arbitrary"),
        name="front",
    )(x, mod, g_pre, w_in, w_map, pool_scale, w_pp)


def _attn_kernel(q_ref, kp_ref, kc_ref, vp_ref, vc_ref, o_ref, lse_ref, *, dil):
    n = pl.program_id(1)
    r = pl.program_id(2)
    s2 = 2 * SPAN
    row = lax.broadcasted_iota(jnp.int32, (s2, s2), 0) % SPAN
    col = lax.broadcasted_iota(jnp.int32, (s2, s2), 1)
    first_row = row + jnp.where(n > 0, 0, s2)
    mask = jnp.where(col < SPAN, col - first_row, row - (col - SPAN)) >= 0
    lane = lax.broadcasted_iota(jnp.int32, (SPAN, LANES), 1)
    low = lane < HEAD_DIM
    rows = pl.ds(r, SPAN, stride=dil) if dil > 1 else slice(None)
    lse_tile = jnp.zeros((SPAN, LANES), F32)
    for j in range(N_SLOTS // 2):
        sl = slice(j * LANES, (j + 1) * LANES)
        q2 = q_ref[:, sl]
        zero = jnp.zeros_like(q2)
        qs = jnp.concatenate([jnp.where(low, q2, zero), jnp.where(low, zero, q2)], axis=0)
        kk = jnp.concatenate([kp_ref[:, sl], kc_ref[:, sl]], axis=0)
        vv = jnp.concatenate([vp_ref[:, sl], vc_ref[:, sl]], axis=0)
        s = lax.dot_general(qs, kk, (((1,), (1,)), ((), ())), preferred_element_type=F32)
        s = jnp.where(mask, s, NEG)
        m = jnp.max(s, axis=-1, keepdims=True)
        p = jnp.exp(s - m)
        den = jnp.sum(p, axis=-1, keepdims=True)
        o2 = _dot(p.astype(BF16), vv) / den
        lse2 = m + jnp.log(den)
        o_ref[j, rows, :] = jnp.where(low, o2[:SPAN], o2[SPAN:])
        lse_tile = jnp.where(lane == 2 * j, lse2[:SPAN], lse_tile)
        lse_tile = jnp.where(lane == 2 * j + 1, lse2[SPAN:], lse_tile)
    lse_ref[rows, :] = lse_tile


def _attn(q, k, v, dil):
    b, _, n_sub, _ = q.shape
    nb = n_sub // SPAN
    t = n_sub * dil
    cur = pl.BlockSpec((None, None, SPAN, D_GROUP), lambda bi, n, r: (bi, r, n, 0))
    prev = pl.BlockSpec((None, None, SPAN, D_GROUP), lambda bi, n, r: (bi, r, jnp.maximum(n - 1, 0), 0))
    return pl.pallas_call(
        functools.partial(_attn_kernel, dil=dil),
        grid=(b, nb, dil),
        in_specs=[cur, prev, cur, prev, cur],
        out_specs=[pl.BlockSpec((None, GROUP_SLABS, SPAN * dil, LANES), lambda bi, n, r: (bi, 0, n, 0)),
                   pl.BlockSpec((None, SPAN * dil, LANES), lambda bi, n, r: (bi, n, 0))],
        out_shape=[jax.ShapeDtypeStruct((b, GROUP_SLABS, t, LANES), F32),
                   jax.ShapeDtypeStruct((b, t, LANES), F32)],
        compiler_params=_params("arbitrary", "arbitrary", "arbitrary"),
        name="attn",
    )(q, k, k, v, v)


def _merge_groups(o_list, lse_list):
    r = lax.broadcasted_iota(jnp.int32, (LANES, D_GROUP), 0)
    c = lax.broadcasted_iota(jnp.int32, (LANES, D_GROUP), 1)
    expand = jnp.where(c // HEAD_DIM == r, 1.0, 0.0).astype(BF16)
    top = jnp.maximum(jnp.maximum(lse_list[0], lse_list[1]), lse_list[2])
    e = [jnp.exp(l - top) for l in lse_list]
    den = e[0] + e[1] + e[2]
    out = None
    for eg, og in zip(e, o_list):
        w = eg / den
        hi = w.astype(BF16)
        lo = (w - hi.astype(F32)).astype(BF16)
        term = (_dot(hi, expand) + _dot(lo, expand)) * og
        out = term if out is None else out + term
    return out


def _gated_ffn(conv_taps, wconv_ref, bconv_ref, wdown_ref, rows):
    acc = jnp.zeros((rows, D_MODEL), F32)
    for c in range(D_FF // FF_CHUNK):
        halves = []
        for base in (0, D_FF):
            cs = slice(base + c * FF_CHUNK, base + (c + 1) * FF_CHUNK)
            hc = bconv_ref[:, cs]
            for j in range(CONV_W):
                hc = hc + conv_taps(j, cs) * wconv_ref[j:j + 1, cs]
            halves.append(hc)
        gated = (_gelu_tanh(halves[0]) * halves[1]).astype(BF16)
        acc = acc + _dot(gated, wdown_ref[c * FF_CHUNK:(c + 1) * FF_CHUNK, :])
    return acc


def _back_kernel(x_ref, mod_ref, gp_ref, sg_ref, o0, o1, o2, l0, l1, l2,
                 wpa_ref, wout_ref, wup_ref, wconv_ref, bconv_ref, wdown_ref,
                 gpost_ref, gpre_ref, gffn_ref, y_ref, ctail_ref, hbuf):
    i = pl.program_id(1)
    tm = x_ref.shape[1]
    mod = lambda r: mod_ref[0, r:r + 1, :]
    o_list = [jnp.concatenate([ref[j] for j in range(GROUP_SLABS)], axis=-1) for ref in (o0, o1, o2)]
    o = _merge_groups(o_list, [l0[0], l1[0], l2[0]])
    y_att = _dot(o.astype(BF16), wpa_ref[...])
    mix = gp_ref[0].astype(F32) + sg_ref[0].astype(F32) * y_att
    x1 = x_ref[0] + mod(2) * _rms(_dot(mix.astype(BF16), wout_ref[...]), gpost_ref[...])
    h2 = (_rms(x1, gpre_ref[...]) * (1.0 + mod(4)) + mod(3)).astype(BF16)

    @pl.when(i == 0)
    def _():
        hbuf[0:CONV_HALO, :] = jnp.zeros((CONV_HALO, 2 * D_FF), F32)

    @pl.when(i > 0)
    def _():
        hbuf[0:CONV_HALO, :] = hbuf[tm:tm + CONV_HALO, :]

    for c in range(2 * D_FF // FF_CHUNK):
        cs = slice(c * FF_CHUNK, (c + 1) * FF_CHUNK)
        hbuf[CONV_HALO:CONV_HALO + tm, cs] = _dot(h2, wup_ref[:, cs])

    def taps(j, cs):
        off = CONV_HALO - (CONV_W - 1) + j
        return hbuf[off:off + tm, cs]

    ffn = _gated_ffn(taps, wconv_ref, bconv_ref, wdown_ref, tm)
    y_ref[0] = x1 + mod(5) * _rms(ffn, gffn_ref[...])
    ctail_ref[0] = hbuf[tm:tm + CONV_HALO, :]


def _back(x, mod, gp, sg, o_list, lse_list, w_pa, w_out, w_up, w_conv, b_conv, w_down,
          g_post, g_pre, g_ffn):
    b, t, d = x.shape
    tm = TM
    row = lambda shape: pl.BlockSpec(shape, lambda bi, i: (bi, i, 0))
    slab = pl.BlockSpec((None, GROUP_SLABS, tm, LANES), lambda bi, i: (bi, 0, i, 0))
    in_specs = ([row((1, tm, d)), pl.BlockSpec((1, 6, d), lambda bi, i: (bi, 0, 0)),
                 row((1, tm, d)), row((1, tm, d))]
                + [slab] * N_GROUPS + [row((1, tm, LANES))] * N_GROUPS
                + [_const_spec(w.shape) for w in (w_pa, w_out, w_up, w_conv, b_conv, w_down)]
                + [_const_spec((1, d))] * 3)
    return pl.pallas_call(
        _back_kernel,
        grid=(b, t // tm),
        in_specs=in_specs,
        out_specs=[row((1, tm, d)), pl.BlockSpec((1, CONV_HALO, 2 * D_FF), lambda bi, i: (bi, 0, 0))],
        out_shape=[jax.ShapeDtypeStruct((b, t, d), F32),
                   jax.ShapeDtypeStruct((b, CONV_HALO, 2 * D_FF), F32)],
        scratch_shapes=[pltpu.VMEM((CONV_HALO + tm, 2 * D_FF), F32)],
        compiler_params=_params("arbitrary", "arbitrary"),
        name="back",
    )(x, mod, gp, sg, *o_list, *lse_list, w_pa, w_out, w_up, w_conv, b_conv, w_down,
      g_post, g_pre, g_ffn)


def _sfront_kernel(x_ref, shift_ref, scale_ref, g_ref, w_ref, o_ref, h_scr):
    @pl.when(pl.program_id(0) == 0)
    def _():
        h = _rms(x_ref[...], g_ref[...]) * (1.0 + scale_ref[...]) + shift_ref[...]
        h_scr[...] = h.astype(BF16)

    o_ref[...] = _dot(h_scr[...], w_ref[...])


def _sfront(x, mod, g_pre, w_in):
    m, d = x.shape
    n = w_in.shape[1]
    tn = 1536
    return pl.pallas_call(
        _sfront_kernel,
        grid=(n // tn,),
        in_specs=[pl.BlockSpec((m, d), lambda j: (0, 0)),
                  pl.BlockSpec((m, d), lambda j: (0, 0)),
                  pl.BlockSpec((m, d), lambda j: (0, 1)),
                  pl.BlockSpec((1, d), lambda j: (0, 0)),
                  pl.BlockSpec((d, tn), lambda j: (0, j))],
        out_specs=pl.BlockSpec((m, tn), lambda j: (0, j)),
        out_shape=jax.ShapeDtypeStruct((m, n), F32),
        scratch_shapes=[pltpu.VMEM((m, d), BF16)],
        compiler_params=_params("arbitrary"),
        name="sfront",
    )(x, mod, mod, g_pre, w_in)


def _sattn_kernel(q_ref, kn_ref, vn_ref, kt_ref, vt_ref, o_ref, lse_ref, kto_ref, vto_ref, *, dil):
    bb, n_slots, _, length = kt_ref.shape
    lane = lax.broadcasted_iota(jnp.int32, (1, length), 1)
    attended = (lane & (dil - 1)) == 0
    last = lane == length - 1

    def per_batch(b, carry):
        for s in range(n_slots):
            q = q_ref[b, :, s:s + 1] * Q_SCALE
            k_new, v_new = kn_ref[b, :, s:s + 1], vn_ref[b, :, s:s + 1]
            kt, vt = kt_ref[b, s], vt_ref[b, s]
            sc = jnp.where(attended, jnp.sum(kt * q, axis=0, keepdims=True), NEG)
            sc_new = jnp.sum(k_new * q, axis=0, keepdims=True)
            m = jnp.maximum(jnp.max(sc, axis=-1, keepdims=True), sc_new)
            p = jnp.exp(sc - m)
            p_new = jnp.exp(sc_new - m)
            den = jnp.sum(p, axis=-1, keepdims=True) + p_new
            o = (jnp.sum(vt * p, axis=-1, keepdims=True) + v_new * p_new) / den
            o_ref[b, :, s:s + 1] = o
            lse_ref[b, :, s:s + 1] = m + jnp.log(den)
            kto_ref[b, s] = jnp.where(last, k_new, pltpu.roll(kt, length - 1, 1))
            vto_ref[b, s] = jnp.where(last, v_new, pltpu.roll(vt, length - 1, 1))
        return carry

    lax.fori_loop(0, bb, per_batch, 0)


def _sattn(q, kn, vn, kt, vt, dil):
    b, n_slots, dh, length = kt.shape
    bb = max(1, CACHE_BLOCK_BYTES // (n_slots * dh * length * 4))
    new = pl.BlockSpec((bb, dh, n_slots), lambda i: (i, 0, 0))
    cache = pl.BlockSpec((bb, n_slots, dh, length), lambda i: (i, 0, 0, 0))
    return pl.pallas_call(
        functools.partial(_sattn_kernel, dil=dil),
        grid=(b // bb,),
        in_specs=[new, new, new, cache, cache],
        out_specs=[new, pl.BlockSpec((bb, 1, n_slots), lambda i: (i, 0, 0)), cache, cache],
        out_shape=[jax.ShapeDtypeStruct((b, dh, n_slots), F32),
                   jax.ShapeDtypeStruct((b, 1, n_slots), F32),
                   jax.ShapeDtypeStruct(kt.shape, F32), jax.ShapeDtypeStruct(vt.shape, F32)],
        compiler_params=_params("arbitrary"),
        name="sattn",
    )(q, kn, vn, kt, vt)


def _pool_state_kernel(hist_ref, u_ref, o_ref):
    newest = pl.program_id(0) == POOL_HIST - 1
    o_ref[0] = jnp.where(newest, u_ref[...], hist_ref[0])


def _pool_state(hist_t, proj):
    steps, m, d = hist_t.shape
    return pl.pallas_call(
        _pool_state_kernel,
        grid=(steps,),
        in_specs=[pl.BlockSpec((1, m, d), lambda j: (jnp.minimum(j + 1, steps - 1), 0, 0)),
                  pl.BlockSpec((m, d), lambda j: (0, COL_U // d))],
        out_specs=pl.BlockSpec((1, m, d), lambda j: (j, 0, 0)),
        out_shape=jax.ShapeDtypeStruct(hist_t.shape, F32),
        compiler_params=_params("arbitrary"),
        name="pool_state",
    )(hist_t, proj)


def _sback_kernel(x_ref, mod_ref, proj_ref, hist_ref, o0, o1, o2, l0, l1, l2, chist_ref,
                  wmap_ref, pscale_ref, wpp_ref, wpa_ref, wout_ref, wup_ref, wconv_ref, bconv_ref,
                  wdown_ref, gpost_ref, gpre_ref, gffn_ref, y_ref, hu_ref):
    m = x_ref.shape[0]
    mod = lambda r: mod_ref[:, r * D_MODEL:(r + 1) * D_MODEL]
    u = proj_ref[:, COL_U:COL_Q]
    mixed = []
    for g, win in enumerate(POOL_WINDOWS):
        lo, hi = g * POOL_GROUP, (g + 1) * POOL_GROUP
        wsum = u[:, lo:hi]
        for j in range(1, win):
            wsum = wsum + hist_ref[POOL_HIST - j, :, lo:hi]
        pooled = wsum / float(win) - u[:, lo:hi]
        mixed.append(_dot(pooled.astype(BF16), wmap_ref[g]))
    mixed = jnp.concatenate(mixed, axis=-1) * pscale_ref[...]
    y_pool = _dot(mixed.astype(BF16), wpp_ref[...])
    o = _merge_groups([o0[...], o1[...], o2[...]], [l0[...], l1[...], l2[...]])
    y_att = _dot(o.astype(BF16), wpa_ref[...])
    mix = (_sigmoid(proj_ref[:, COL_ZP:COL_ZA]) * y_pool + _sigmoid(proj_ref[:, COL_ZA:D_IN]) * y_att)
    x1 = x_ref[...] + mod(2) * _rms(_dot(mix.astype(BF16), wout_ref[...]), gpost_ref[...])
    h2 = (_rms(x1, gpre_ref[...]) * (1.0 + mod(4)) + mod(3)).astype(BF16)

    for c in range(2 * D_FF // FF_CHUNK):
        cs = slice(c * FF_CHUNK, (c + 1) * FF_CHUNK)
        hu_ref[:, cs] = _dot(h2, wup_ref[:, cs])

    def taps(j, cs):
        return hu_ref[:, cs] if j == CONV_W - 1 else chist_ref[j, :, cs]

    ffn = _gated_ffn(taps, wconv_ref, bconv_ref, wdown_ref, m)
    y_ref[...] = x1 + mod(5) * _rms(ffn, gffn_ref[...])


def _sback(x, mod, proj, hist_t, o_list, lse_list, chist_t, weights, gains):
    m, d = x.shape
    args = [x, mod, proj, hist_t, *o_list, *lse_list, chist_t, *weights, *gains]
    full = lambda shape: pl.BlockSpec(shape, lambda i: (0,) * len(shape))
    out_shapes = [(m, d), (m, 2 * D_FF)]
    return pl.pallas_call(
        _sback_kernel,
        grid=(1,),
        in_specs=[_const_spec(a.shape) for a in args],
        out_specs=[full(s) for s in out_shapes],
        out_shape=[jax.ShapeDtypeStruct(s, F32) for s in out_shapes],
        compiler_params=_params("arbitrary"),
        name="sback",
    )(*args)


def kernel(x_prompt, x_sample, c_prompt, c_sample, cache_k_w128, cache_v_w128, cache_k_w512, cache_v_w512,
           cache_k_w2048, cache_v_w2048, state_pool, state_conv, w_ada, b_ada, g_pre_mix, g_post_mix,
           g_pre_ffn, g_post_ffn, w_in, w_pool_map, pool_scale, w_proj_pool, w_proj_att, w_out, w_up,
           w_conv, b_conv, w_down):
    bp, tp, d = x_prompt.shape
    bs = x_sample.shape[0]
    (w_ada, b_ada, g_pre_mix, g_post_mix, g_pre_ffn, g_post_ffn, w_in, w_pool_map, pool_scale,
     w_proj_pool, w_proj_att, w_out, w_up, w_conv, b_conv, w_down) = (
        w[0] for w in (w_ada, b_ada, g_pre_mix, g_post_mix, g_pre_ffn, g_post_ffn, w_in, w_pool_map,
                       pool_scale, w_proj_pool, w_proj_att, w_out, w_up, w_conv, b_conv, w_down))
    caches = [(cache_k_w128[0], cache_v_w128[0]), (cache_k_w512[0], cache_v_w512[0]),
              (cache_k_w2048[0], cache_v_w2048[0])]
    g_pre_mix, g_post_mix, g_pre_ffn, g_post_ffn, pool_scale, b_conv, b_ada = (
        a.reshape(1, -1) for a in (g_pre_mix, g_post_mix, g_pre_ffn, g_post_ffn, pool_scale, b_conv, b_ada))
    w_in_b, w_map_b, w_pp_b, w_pa_b, w_out_b, w_up_b, w_down_b = (
        w.astype(BF16) for w in (w_in, w_pool_map, w_proj_pool, w_proj_att, w_out, w_up, w_down))

    mod = _ada(jnp.concatenate([c_prompt, c_sample], axis=0), w_ada, b_ada)
    mod_p = mod[:bp].reshape(bp, 6, d)
    mod_s = mod[bp:]

    proj_s = _sfront(x_sample[:, 0], mod_s, g_pre_mix, w_in_b)
    columns = lambda lo: proj_s[:, lo:lo + D_ATT].reshape(bs, N_GROUPS, N_SLOTS, HEAD_DIM).transpose(1, 0, 3, 2)
    q_s, k_s, v_s = columns(COL_Q), columns(COL_K), columns(COL_V)
    o_s, lse_s, new_caches = [], [], []
    for g, ((_, dil), (kc, vc)) in enumerate(zip(ATT_GROUPS, caches)):
        kt, vt = kc.transpose(0, 2, 3, 1), vc.transpose(0, 2, 3, 1)
        o_g, lse_g, kt_new, vt_new = _sattn(q_s[g], k_s[g], v_s[g], kt, vt, dil)
        o_s.append(o_g.transpose(0, 2, 1).reshape(bs, D_GROUP))
        lse_s.append(jnp.pad(lse_g[:, 0], ((0, 0), (0, LANES - N_SLOTS))))
        new_caches += [kt_new.transpose(0, 3, 1, 2)[None], vt_new.transpose(0, 3, 1, 2)[None]]
    hist_t = state_pool[0].transpose(1, 0, 2)
    chist_t = state_conv[0].transpose(1, 0, 2)
    pool_state_t = _pool_state(hist_t, proj_s)
    y_s, hu_s = _sback(
        x_sample[:, 0], mod_s, proj_s, hist_t, o_s, lse_s, chist_t,
        (w_map_b, pool_scale, w_pp_b, w_pa_b, w_out_b, w_up_b, w_conv, b_conv, w_down_b),
        (g_post_mix, g_pre_ffn, g_post_ffn))
    conv_state_s = jnp.concatenate([state_conv[0][:, 1:], hu_s[:, None]], axis=1)

    front = _front(x_prompt, mod_p, g_pre_mix, w_in_b, w_map_b, pool_scale, w_pp_b)
    gp, sg = front[0], front[1]
    qkv = front[2:11]
    kv_t = front[11:17]
    u_tail = front[17]
    o_list, lse_list = [], []
    for g, (_, dil) in enumerate(ATT_GROUPS):
        o_g, lse_g = _attn(qkv[3 * g], qkv[3 * g + 1], qkv[3 * g + 2], dil)
        o_list.append(o_g)
        lse_list.append(lse_g)
    y_p, conv_tail = _back(x_prompt, mod_p, gp, sg, o_list, lse_list, w_pa_b, w_out_b, w_up_b, w_conv,
                           b_conv, w_down_b, g_post_mix, g_pre_ffn, g_post_ffn)

    outs = [y_p, y_s[:, None]]
    for g in range(N_GROUPS):
        for j in range(2):
            f = kv_t[2 * g + j]
            outs.append(f.reshape(bp, N_SLOTS, HEAD_DIM, f.shape[2]).transpose(0, 3, 1, 2)[None])
            outs.append(new_caches[2 * g + j])
    outs += [u_tail[None, :, HALO - POOL_HIST:], pool_state_t.transpose(1, 0, 2)[None],
             conv_tail[None, :, CONV_HALO - (CONV_W - 1):], conv_state_s[None]]
    return tuple(outs)
```

```python
import functools

import jax
import jax.numpy as jnp
from jax import lax
from jax.experimental import pallas as pl
from jax.experimental.pallas import tpu as pltpu

F32 = jnp.float32
BF16 = jnp.bfloat16

D_MODEL = 1024
EPS = 1e-6
POOL_WINDOWS = (2, 4, 8, 16)
POOL_GROUP = D_MODEL // len(POOL_WINDOWS)
POOL_HIST = max(POOL_WINDOWS) - 1
ATT_GROUPS = ((128, 1), (512, 4), (2048, 16))
SPAN = 128
N_GROUPS = len(ATT_GROUPS)
N_SLOTS = 8
HEAD_DIM = 64
D_GROUP = N_SLOTS * HEAD_DIM
D_ATT = N_GROUPS * D_GROUP
D_FF = 2816
CONV_W = 3
COL_U, COL_Q, COL_K, COL_V, COL_ZP, COL_ZA, D_IN = 0, 1024, 2560, 4096, 5632, 6656, 7680
Q_SCALE = HEAD_DIM ** -0.5
NEG = -1e30

LANES = 128
GROUP_SLABS = D_GROUP // LANES
HALO = 16
CONV_HALO = 8
TM = 256
FF_CHUNK = 1408
VMEM_LIMIT = 56 * 1024 * 1024
CACHE_BLOCK_BYTES = 4 * 1024 * 1024


def _const_spec(shape):
    nd = len(shape)
    return pl.BlockSpec(shape, lambda *_: (0,) * nd, pipeline_mode=pl.Buffered(1))


def _rms(x, g):
    return x * lax.rsqrt(jnp.mean(x * x, axis=-1, keepdims=True) + EPS) * g


def _sigmoid(x):
    return 1.0 / (1.0 + jnp.exp(-x))


def _gelu_tanh(x):
    return 0.5 * x * (1.0 + jnp.tanh(0.7978845608028654 * (x + 0.044715 * (x * x * x))))


def _dot(a, b):
    return jnp.dot(a, b, preferred_element_type=F32)


def _params(*semantics):
    return pltpu.CompilerParams(dimension_semantics=semantics, vmem_limit_bytes=VMEM_LIMIT)


def _ada_kernel(c_ref, w_ref, b_ref, o_ref):
    c = c_ref[...]
    s = c * _sigmoid(c)
    o_ref[...] = _dot(s.astype(BF16), w_ref[...].astype(BF16)) + b_ref[...]


def _ada(c_all, w_ada, b_ada):
    m = c_all.shape[0]
    n = w_ada.shape[1]
    tn = 1536
    return pl.pallas_call(
        _ada_kernel,
        grid=(n // tn,),
        in_specs=[pl.BlockSpec((m, D_MODEL), lambda j: (0, 0)),
                  pl.BlockSpec((D_MODEL, tn), lambda j: (0, j)),
                  pl.BlockSpec((1, tn), lambda j: (0, j))],
        out_specs=pl.BlockSpec((m, tn), lambda j: (0, j)),
        out_shape=jax.ShapeDtypeStruct((m, n), F32),
        compiler_params=_params("arbitrary"),
        name="ada",
    )(c_all, w_ada, b_ada)


def _front_kernel(x_ref, mod_ref, g_ref, win_ref, wmap_ref, pscale_ref, wpp_ref,
                  gp_ref, sg_ref, q0, k0, v0, q1, k1, v1, q2, k2, v2,
                  kf0, vf0, kf1, vf1, kf2, vf2, utail_ref, ubuf, slabs):
    i = pl.program_id(1)
    tm = x_ref.shape[1]
    x = x_ref[0]
    shift = mod_ref[0, 0:1, :]
    scale = mod_ref[0, 1:2, :]
    h = (_rms(x, g_ref[...]) * (1.0 + scale) + shift).astype(BF16)

    u = _dot(h, win_ref[:, COL_U:COL_Q])

    @pl.when(i == 0)
    def _():
        ubuf[0:HALO, :] = jnp.zeros((HALO, D_MODEL), F32)

    @pl.when(i > 0)
    def _():
        ubuf[0:HALO, :] = ubuf[tm:tm + HALO, :]

    ubuf[HALO:HALO + tm, :] = u
    utail_ref[0] = u[tm - HALO:, :]

    pos = i * tm + lax.broadcasted_iota(jnp.int32, (tm, 1), 0)
    mixed = []
    for g, win in enumerate(POOL_WINDOWS):
        lo, hi = g * POOL_GROUP, (g + 1) * POOL_GROUP
        wsum = ubuf[HALO:HALO + tm, lo:hi]
        for j in range(1, win):
            wsum = wsum + ubuf[HALO - j:HALO - j + tm, lo:hi]
        count = jnp.minimum(pos + 1, win).astype(F32)
        pooled = wsum / count - ubuf[HALO:HALO + tm, lo:hi]
        mixed.append(_dot(pooled.astype(BF16), wmap_ref[g]))
    mixed = jnp.concatenate(mixed, axis=-1) * pscale_ref[...]
    y_pool = _dot(mixed.astype(BF16), wpp_ref[...])
    gp_ref[0] = (_sigmoid(_dot(h, win_ref[:, COL_ZP:COL_ZA])) * y_pool).astype(BF16)
    sg_ref[0] = _sigmoid(_dot(h, win_ref[:, COL_ZA:D_IN])).astype(BF16)

    def emit(val, refs, frefs):
        for j in range(D_ATT // LANES):
            slabs[j] = val[:, j * LANES:(j + 1) * LANES]
        for g, (_, dil) in enumerate(ATT_GROUPS):
            if dil == 1:
                refs[g][0, 0] = val[:, g * D_GROUP:(g + 1) * D_GROUP].astype(BF16)
                continue
            for r in range(dil):
                for jj in range(GROUP_SLABS):
                    piece = slabs[g * GROUP_SLABS + jj, pl.ds(r, tm // dil, stride=dil), :]
                    refs[g][0, r, :, jj * LANES:(jj + 1) * LANES] = piece.astype(BF16)
        for g, fref in enumerate(frefs):
            rows = fref.shape[2]
            fref[0] = val[tm - rows:, g * D_GROUP:(g + 1) * D_GROUP].T

    emit(_dot(h, win_ref[:, COL_Q:COL_K]) * Q_SCALE, (q0, q1, q2), ())
    emit(_dot(h, win_ref[:, COL_K:COL_V]), (k0, k1, k2), (kf0, kf1, kf2))
    emit(_dot(h, win_ref[:, COL_V:COL_ZP]), (v0, v1, v2), (vf0, vf1, vf2))


def _front(x, mod, g_pre, w_in, w_map, pool_scale, w_pp):
    b, t, d = x.shape
    tm = TM
    nblk = t // tm
    row = lambda shape: pl.BlockSpec(shape, lambda bi, i: (bi, i, 0))
    in_specs = [row((1, tm, d)),
                pl.BlockSpec((1, 6, d), lambda bi, i: (bi, 0, 0)),
                _const_spec((1, d)), _const_spec(w_in.shape), _const_spec(w_map.shape),
                _const_spec((1, d)), _const_spec(w_pp.shape)]
    out_shape = [jax.ShapeDtypeStruct((b, t, d), BF16), jax.ShapeDtypeStruct((b, t, d), BF16)]
    out_specs = [row((1, tm, d)), row((1, tm, d))]
    for _, dil in ATT_GROUPS:
        for _ in range(3):
            out_shape.append(jax.ShapeDtypeStruct((b, dil, t // dil, D_GROUP), BF16))
            out_specs.append(pl.BlockSpec((1, dil, tm // dil, D_GROUP), lambda bi, i: (bi, 0, i, 0)))
    for win, _ in ATT_GROUPS:
        keep = min(win, t)
        rows = min(keep, tm)
        first = nblk - keep // rows
        for _ in range(2):
            out_shape.append(jax.ShapeDtypeStruct((b, D_GROUP, keep), F32))
            out_specs.append(pl.BlockSpec(
                (1, D_GROUP, rows), lambda bi, i, first=first: (bi, 0, jnp.maximum(i - first, 0))))
    out_shape.append(jax.ShapeDtypeStruct((b, HALO, d), F32))
    out_specs.append(pl.BlockSpec((1, HALO, d), lambda bi, i: (bi, 0, 0)))
    return pl.pallas_call(
        _front_kernel,
        grid=(b, nblk),
        in_specs=in_specs, out_specs=out_specs, out_shape=out_shape,
        scratch_shapes=[pltpu.VMEM((HALO + tm, d), F32),
                        pltpu.VMEM((D_ATT // LANES, tm, LANES), F32)],
        compiler_params=_params("arbitrary", "arbitrary"),
        name="front",
    )(x, mod, g_pre, w_in, w_map, pool_scale, w_pp)


def _attn_unit(q_ref, kp_ref, kc_ref, vp_ref, vc_ref, o_ref, lse_ref, n, r, dil):
    s2 = 2 * SPAN
    row = lax.broadcasted_iota(jnp.int32, (s2, s2), 0) % SPAN
    col = lax.broadcasted_iota(jnp.int32, (s2, s2), 1)
    first_row = row + jnp.where(n > 0, 0, s2)
    mask = jnp.where(col < SPAN, col - first_row, row - (col - SPAN)) >= 0
    lane = lax.broadcasted_iota(jnp.int32, (SPAN, LANES), 1)
    low = lane < HEAD_DIM
    rows = pl.ds(r, SPAN, stride=dil) if dil > 1 else slice(None)
    lse_tile = jnp.zeros((SPAN, LANES), F32)
    for j in range(N_SLOTS // 2):
        sl = slice(j * LANES, (j + 1) * LANES)
        q2 = q_ref[:, sl]
        zero = jnp.zeros_like(q2)
        qs = jnp.concatenate([jnp.where(low, q2, zero), jnp.where(low, zero, q2)], axis=0)
        kk = jnp.concatenate([kp_ref[:, sl], kc_ref[:, sl]], axis=0)
        vv = jnp.concatenate([vp_ref[:, sl], vc_ref[:, sl]], axis=0)
        s = lax.dot_general(qs, kk, (((1,), (1,)), ((), ())), preferred_element_type=F32)
        s = jnp.where(mask, s, NEG)
        m = jnp.max(s, axis=-1, keepdims=True)
        p = jnp.exp(s - m)
        den = jnp.sum(p, axis=-1, keepdims=True)
        o2 = _dot(p.astype(BF16), vv) / den
        lse2 = m + jnp.log(den)
        o_ref[j, rows, :] = jnp.where(low, o2[:SPAN], o2[SPAN:])
        lse_tile = jnp.where(lane == 2 * j, lse2[:SPAN], lse_tile)
        lse_tile = jnp.where(lane == 2 * j + 1, lse2[SPAN:], lse_tile)
    lse_ref[rows, :] = lse_tile


def _attn_unit_specs(q, dil):
    b, _, n_sub, _ = q.shape
    nb = n_sub // SPAN
    t = n_sub * dil
    unit = lambda i: (i // (nb * dil), (i // dil) % nb, i % dil)

    def cur(i):
        bi, n, r = unit(i)
        return bi, r, n, 0

    def prev(i):
        bi, n, r = unit(i)
        return bi, r, jnp.maximum(n - 1, 0), 0

    blk = lambda f: pl.BlockSpec((None, None, SPAN, D_GROUP), f)
    in_specs = [blk(cur), blk(prev), blk(cur), blk(prev), blk(cur)]
    out_specs = [pl.BlockSpec((None, GROUP_SLABS, SPAN * dil, LANES), lambda i: (unit(i)[0], 0, unit(i)[1], 0)),
                 pl.BlockSpec((None, SPAN * dil, LANES), lambda i: (unit(i)[0], unit(i)[1], 0))]
    out_shape = [jax.ShapeDtypeStruct((b, GROUP_SLABS, t, LANES), F32),
                 jax.ShapeDtypeStruct((b, t, LANES), F32)]
    return b * nb * dil, nb, in_specs, out_specs, out_shape


def _merge_groups(o_list, lse_list):
    r = lax.broadcasted_iota(jnp.int32, (LANES, D_GROUP), 0)
    c = lax.broadcasted_iota(jnp.int32, (LANES, D_GROUP), 1)
    expand = jnp.where(c // HEAD_DIM == r, 1.0, 0.0).astype(BF16)
    top = jnp.maximum(jnp.maximum(lse_list[0], lse_list[1]), lse_list[2])
    e = [jnp.exp(l - top) for l in lse_list]
    den = e[0] + e[1] + e[2]
    out = None
    for eg, og in zip(e, o_list):
        w = eg / den
        hi = w.astype(BF16)
        lo = (w - hi.astype(F32)).astype(BF16)
        term = (_dot(hi, expand) + _dot(lo, expand)) * og
        out = term if out is None else out + term
    return out


def _gated_ffn(conv_taps, wconv_ref, bconv_ref, wdown_ref, rows):
    acc = jnp.zeros((rows, D_MODEL), F32)
    for c in range(D_FF // FF_CHUNK):
        halves = []
        for base in (0, D_FF):
            cs = slice(base + c * FF_CHUNK, base + (c + 1) * FF_CHUNK)
            hc = bconv_ref[:, cs]
            for j in range(CONV_W):
                hc = hc + conv_taps(j, cs) * wconv_ref[j:j + 1, cs]
            halves.append(hc)
        gated = (_gelu_tanh(halves[0]) * halves[1]).astype(BF16)
        acc = acc + _dot(gated, wdown_ref[c * FF_CHUNK:(c + 1) * FF_CHUNK, :])
    return acc


def _back_kernel(x_ref, mod_ref, gp_ref, sg_ref, o0, o1, o2, l0, l1, l2,
                 wpa_ref, wout_ref, wup_ref, wconv_ref, bconv_ref, wdown_ref,
                 gpost_ref, gpre_ref, gffn_ref, y_ref, ctail_ref, hbuf):
    i = pl.program_id(1)
    tm = x_ref.shape[1]
    mod = lambda r: mod_ref[0, r:r + 1, :]
    o_list = [jnp.concatenate([ref[j] for j in range(GROUP_SLABS)], axis=-1) for ref in (o0, o1, o2)]
    o = _merge_groups(o_list, [l0[0], l1[0], l2[0]])
    y_att = _dot(o.astype(BF16), wpa_ref[...])
    mix = gp_ref[0].astype(F32) + sg_ref[0].astype(F32) * y_att
    x1 = x_ref[0] + mod(2) * _rms(_dot(mix.astype(BF16), wout_ref[...]), gpost_ref[...])
    h2 = (_rms(x1, gpre_ref[...]) * (1.0 + mod(4)) + mod(3)).astype(BF16)

    @pl.when(i == 0)
    def _():
        hbuf[0:CONV_HALO, :] = jnp.zeros((CONV_HALO, 2 * D_FF), F32)

    @pl.when(i > 0)
    def _():
        hbuf[0:CONV_HALO, :] = hbuf[tm:tm + CONV_HALO, :]

    for c in range(2 * D_FF // FF_CHUNK):
        cs = slice(c * FF_CHUNK, (c + 1) * FF_CHUNK)
        hbuf[CONV_HALO:CONV_HALO + tm, cs] = _dot(h2, wup_ref[:, cs])

    def taps(j, cs):
        off = CONV_HALO - (CONV_W - 1) + j
        return hbuf[off:off + tm, cs]

    ffn = _gated_ffn(taps, wconv_ref, bconv_ref, wdown_ref, tm)
    y_ref[0] = x1 + mod(5) * _rms(ffn, gffn_ref[...])
    ctail_ref[0] = hbuf[tm:tm + CONV_HALO, :]


def _back(x, mod, gp, sg, o_list, lse_list, w_pa, w_out, w_up, w_conv, b_conv, w_down,
          g_post, g_pre, g_ffn):
    b, t, d = x.shape
    tm = TM
    row = lambda shape: pl.BlockSpec(shape, lambda bi, i: (bi, i, 0))
    slab = pl.BlockSpec((None, GROUP_SLABS, tm, LANES), lambda bi, i: (bi, 0, i, 0))
    in_specs = ([row((1, tm, d)), pl.BlockSpec((1, 6, d), lambda bi, i: (bi, 0, 0)),
                 row((1, tm, d)), row((1, tm, d))]
                + [slab] * N_GROUPS + [row((1, tm, LANES))] * N_GROUPS
                + [_const_spec(w.shape) for w in (w_pa, w_out, w_up, w_conv, b_conv, w_down)]
                + [_const_spec((1, d))] * 3)
    return pl.pallas_call(
        _back_kernel,
        grid=(b, t // tm),
        in_specs=in_specs,
        out_specs=[row((1, tm, d)), pl.BlockSpec((1, CONV_HALO, 2 * D_FF), lambda bi, i: (bi, 0, 0))],
        out_shape=[jax.ShapeDtypeStruct((b, t, d), F32),
                   jax.ShapeDtypeStruct((b, CONV_HALO, 2 * D_FF), F32)],
        scratch_shapes=[pltpu.VMEM((CONV_HALO + tm, 2 * D_FF), F32)],
        compiler_params=_params("arbitrary", "arbitrary"),
        name="back",
    )(x, mod, gp, sg, *o_list, *lse_list, w_pa, w_out, w_up, w_conv, b_conv, w_down,
      g_post, g_pre, g_ffn)


def _sfront_kernel(x_ref, shift_ref, scale_ref, g_ref, w_ref, o_ref, h_scr):
    @pl.when(pl.program_id(0) == 0)
    def _():
        h = _rms(x_ref[...], g_ref[...]) * (1.0 + scale_ref[...]) + shift_ref[...]
        h_scr[...] = h.astype(BF16)

    o_ref[...] = _dot(h_scr[...], w_ref[...])


def _sfront(x, mod, g_pre, w_in):
    m, d = x.shape
    n = w_in.shape[1]
    tn = 1536
    return pl.pallas_call(
        _sfront_kernel,
        grid=(n // tn,),
        in_specs=[pl.BlockSpec((m, d), lambda j: (0, 0)),
                  pl.BlockSpec((m, d), lambda j: (0, 0)),
                  pl.BlockSpec((m, d), lambda j: (0, 1)),
                  pl.BlockSpec((1, d), lambda j: (0, 0)),
                  pl.BlockSpec((d, tn), lambda j: (0, j))],
        out_specs=pl.BlockSpec((m, tn), lambda j: (0, j)),
        out_shape=jax.ShapeDtypeStruct((m, n), F32),
        scratch_shapes=[pltpu.VMEM((m, d), BF16)],
        compiler_params=_params("arbitrary"),
        name="sfront",
    )(x, mod, mod, g_pre, w_in)


def _sattn_kernel(*refs, dil, units):
    q_ref, kn_ref, vn_ref, kt_ref, vt_ref = refs[:5]
    n_in = 5 + 5 * len(units)
    o_ref, lse_ref, kto_ref, vto_ref = refs[n_in:n_in + 4]
    step = pl.program_id(0)
    for g, (nb, unit_dil) in enumerate(units):
        _attn_unit(*refs[5 + 5 * g:10 + 5 * g], *refs[n_in + 4 + 2 * g:n_in + 6 + 2 * g],
                   (step // unit_dil) % nb, step % unit_dil, unit_dil)

    bb, n_slots, _, length = kt_ref.shape
    lane = lax.broadcasted_iota(jnp.int32, (1, length), 1)
    attended = (lane & (dil - 1)) == 0
    last = lane == length - 1

    slot = lax.broadcasted_iota(jnp.int32, (n_slots, 1), 0)

    def per_batch(b, carry):
        q = q_ref[b] * Q_SCALE
        k_new, v_new = kn_ref[b], vn_ref[b]
        sc = jnp.zeros((n_slots, length), F32)
        sc_new = jnp.zeros((n_slots, 1), F32)
        for s in range(n_slots):
            col = slice(s, s + 1)
            kt = kt_ref[b, s]
            sc = jnp.where(slot == s, jnp.sum(kt * q[:, col], axis=0, keepdims=True), sc)
            sc_new = jnp.where(slot == s, jnp.sum(k_new[:, col] * q[:, col], axis=0, keepdims=True), sc_new)
            kto_ref[b, s] = jnp.where(last, k_new[:, col], pltpu.roll(kt, length - 1, 1))
        sc = jnp.where(attended, sc, NEG)
        m = jnp.maximum(jnp.max(sc, axis=-1, keepdims=True), sc_new)
        p = jnp.exp(sc - m)
        p_new = jnp.exp(sc_new - m)
        den = jnp.sum(p, axis=-1, keepdims=True) + p_new
        lse_ref[b] = m + jnp.log(den)
        inv = 1.0 / den
        for s in range(n_slots):
            col, row = slice(s, s + 1), slice(s, s + 1)
            vt = vt_ref[b, s]
            o = jnp.sum(vt * p[row], axis=-1, keepdims=True) + v_new[:, col] * p_new[row]
            o_ref[b, :, col] = o * inv[row]
            vto_ref[b, s] = jnp.where(last, v_new[:, col], pltpu.roll(vt, length - 1, 1))
        return carry

    lax.fori_loop(0, bb, per_batch, 0, unroll=min(bb, 4))


def _sattn(q, kn, vn, kt, vt, dil, prompt=()):
    b, n_slots, dh, length = kt.shape
    bb = max(1, CACHE_BLOCK_BYTES // (n_slots * dh * length * 4))
    steps = b // bb
    new = pl.BlockSpec((bb, dh, n_slots), lambda i: (i, 0, 0))
    cache = pl.BlockSpec((bb, n_slots, dh, length), lambda i: (i, 0, 0, 0))
    in_specs = [new, new, new, cache, cache]
    out_specs = [new, pl.BlockSpec((bb, n_slots, 1), lambda i: (i, 0, 0)), cache, cache]
    out_shape = [jax.ShapeDtypeStruct((b, dh, n_slots), F32),
                 jax.ShapeDtypeStruct((b, n_slots, 1), F32),
                 jax.ShapeDtypeStruct(kt.shape, F32), jax.ShapeDtypeStruct(vt.shape, F32)]
    args = [q, kn, vn, kt, vt]
    units = []
    for pq, pk, pv, pdil in prompt:
        n_units, nb, ins, outs, shapes = _attn_unit_specs(pq, pdil)
        assert n_units == steps, (n_units, steps)
        units.append((nb, pdil))
        in_specs += ins
        out_specs += outs
        out_shape += shapes
        args += [pq, pk, pk, pv, pv]
    return pl.pallas_call(
        functools.partial(_sattn_kernel, dil=dil, units=tuple(units)),
        grid=(steps,),
        in_specs=in_specs, out_specs=out_specs, out_shape=out_shape,
        compiler_params=_params("arbitrary"),
        name="sattn",
    )(*args)


def _pool_state_kernel(hist_ref, u_ref, o_ref):
    newest = pl.program_id(0) == POOL_HIST - 1
    o_ref[0] = jnp.where(newest, u_ref[...], hist_ref[0])


def _pool_state(hist_t, proj):
    steps, m, d = hist_t.shape
    return pl.pallas_call(
        _pool_state_kernel,
        grid=(steps,),
        in_specs=[pl.BlockSpec((1, m, d), lambda j: (jnp.minimum(j + 1, steps - 1), 0, 0)),
                  pl.BlockSpec((m, d), lambda j: (0, COL_U // d))],
        out_specs=pl.BlockSpec((1, m, d), lambda j: (j, 0, 0)),
        out_shape=jax.ShapeDtypeStruct(hist_t.shape, F32),
        compiler_params=_params("arbitrary"),
        name="pool_state",
    )(hist_t, proj)


def _sback_kernel(x_ref, mod_ref, proj_ref, hist_ref, o0, o1, o2, l0, l1, l2, chist_ref,
                  wmap_ref, pscale_ref, wpp_ref, wpa_ref, wout_ref, wup_ref, wconv_ref, bconv_ref,
                  wdown_ref, gpost_ref, gpre_ref, gffn_ref, y_ref, hu_ref):
    m = x_ref.shape[0]
    mod = lambda r: mod_ref[:, r * D_MODEL:(r + 1) * D_MODEL]
    u = proj_ref[:, COL_U:COL_Q]
    mixed = []
    for g, win in enumerate(POOL_WINDOWS):
        lo, hi = g * POOL_GROUP, (g + 1) * POOL_GROUP
        wsum = u[:, lo:hi]
        for j in range(1, win):
            wsum = wsum + hist_ref[POOL_HIST - j, :, lo:hi]
        pooled = wsum / float(win) - u[:, lo:hi]
        mixed.append(_dot(pooled.astype(BF16), wmap_ref[g]))
    mixed = jnp.concatenate(mixed, axis=-1) * pscale_ref[...]
    y_pool = _dot(mixed.astype(BF16), wpp_ref[...])
    o = _merge_groups([o0[...], o1[...], o2[...]], [l0[...], l1[...], l2[...]])
    y_att = _dot(o.astype(BF16), wpa_ref[...])
    mix = (_sigmoid(proj_ref[:, COL_ZP:COL_ZA]) * y_pool + _sigmoid(proj_ref[:, COL_ZA:D_IN]) * y_att)
    x1 = x_ref[...] + mod(2) * _rms(_dot(mix.astype(BF16), wout_ref[...]), gpost_ref[...])
    h2 = (_rms(x1, gpre_ref[...]) * (1.0 + mod(4)) + mod(3)).astype(BF16)

    for c in range(2 * D_FF // FF_CHUNK):
        cs = slice(c * FF_CHUNK, (c + 1) * FF_CHUNK)
        hu_ref[:, cs] = _dot(h2, wup_ref[:, cs])

    def taps(j, cs):
        return hu_ref[:, cs] if j == CONV_W - 1 else chist_ref[j, :, cs]

    ffn = _gated_ffn(taps, wconv_ref, bconv_ref, wdown_ref, m)
    y_ref[...] = x1 + mod(5) * _rms(ffn, gffn_ref[...])


def _sback(x, mod, proj, hist_t, o_list, lse_list, chist_t, weights, gains):
    m, d = x.shape
    args = [x, mod, proj, hist_t, *o_list, *lse_list, chist_t, *weights, *gains]
    full = lambda shape: pl.BlockSpec(shape, lambda i: (0,) * len(shape))
    out_shapes = [(m, d), (m, 2 * D_FF)]
    return pl.pallas_call(
        _sback_kernel,
        grid=(1,),
        in_specs=[_const_spec(a.shape) for a in args],
        out_specs=[full(s) for s in out_shapes],
        out_shape=[jax.ShapeDtypeStruct(s, F32) for s in out_shapes],
        compiler_params=_params("arbitrary"),
        name="sback",
    )(*args)


def kernel(x_prompt, x_sample, c_prompt, c_sample, cache_k_w128, cache_v_w128, cache_k_w512, cache_v_w512,
           cache_k_w2048, cache_v_w2048, state_pool, state_conv, w_ada, b_ada, g_pre_mix, g_post_mix,
           g_pre_ffn, g_post_ffn, w_in, w_pool_map, pool_scale, w_proj_pool, w_proj_att, w_out, w_up,
           w_conv, b_conv, w_down):
    bp, tp, d = x_prompt.shape
    bs = x_sample.shape[0]
    (w_ada, b_ada, g_pre_mix, g_post_mix, g_pre_ffn, g_post_ffn, w_in, w_pool_map, pool_scale,
     w_proj_pool, w_proj_att, w_out, w_up, w_conv, b_conv, w_down) = (
        w[0] for w in (w_ada, b_ada, g_pre_mix, g_post_mix, g_pre_ffn, g_post_ffn, w_in, w_pool_map,
                       pool_scale, w_proj_pool, w_proj_att, w_out, w_up, w_conv, b_conv, w_down))
    caches = [(cache_k_w128[0], cache_v_w128[0]), (cache_k_w512[0], cache_v_w512[0]),
              (cache_k_w2048[0], cache_v_w2048[0])]
    g_pre_mix, g_post_mix, g_pre_ffn, g_post_ffn, pool_scale, b_conv, b_ada = (
        a.reshape(1, -1) for a in (g_pre_mix, g_post_mix, g_pre_ffn, g_post_ffn, pool_scale, b_conv, b_ada))
    w_in_b, w_map_b, w_pp_b, w_pa_b, w_out_b, w_up_b, w_down_b = (
        w.astype(BF16) for w in (w_in, w_pool_map, w_proj_pool, w_proj_att, w_out, w_up, w_down))

    mod = _ada(jnp.concatenate([c_prompt, c_sample], axis=0), w_ada, b_ada)
    mod_p = mod[:bp].reshape(bp, 6, d)
    mod_s = mod[bp:]

    front = _front(x_prompt, mod_p, g_pre_mix, w_in_b, w_map_b, pool_scale, w_pp_b)
    gp, sg = front[0], front[1]
    qkv = front[2:11]
    kv_t = front[11:17]
    u_tail = front[17]
    prompt_units = [(qkv[3 * g], qkv[3 * g + 1], qkv[3 * g + 2], dil) for g, (_, dil) in enumerate(ATT_GROUPS)]

    proj_s = _sfront(x_sample[:, 0], mod_s, g_pre_mix, w_in_b)
    columns = lambda lo: proj_s[:, lo:lo + D_ATT].reshape(bs, N_GROUPS, N_SLOTS, HEAD_DIM).transpose(1, 0, 3, 2)
    q_s, k_s, v_s = columns(COL_Q), columns(COL_K), columns(COL_V)
    o_s, lse_s, new_caches = [], [], []
    for g, ((_, dil), (kc, vc)) in enumerate(zip(ATT_GROUPS, caches)):
        kt, vt = kc.transpose(0, 2, 3, 1), vc.transpose(0, 2, 3, 1)
        if g == N_GROUPS - 1:
            res = _sattn(q_s[g], k_s[g], v_s[g], kt, vt, dil, prompt_units)
            o_list, lse_list = list(res[4::2]), list(res[5::2])
        else:
            res = _sattn(q_s[g], k_s[g], v_s[g], kt, vt, dil)
        o_g, lse_g, kt_new, vt_new = res[:4]
        o_s.append(o_g.transpose(0, 2, 1).reshape(bs, D_GROUP))
        lse_s.append(jnp.pad(lse_g[:, :, 0], ((0, 0), (0, LANES - N_SLOTS))))
        new_caches += [kt_new.transpose(0, 3, 1, 2)[None], vt_new.transpose(0, 3, 1, 2)[None]]
    hist_t = state_pool[0].transpose(1, 0, 2)
    chist_t = state_conv[0].transpose(1, 0, 2)
    pool_state_t = _pool_state(hist_t, proj_s)
    y_s, hu_s = _sback(
        x_sample[:, 0], mod_s, proj_s, hist_t, o_s, lse_s, chist_t,
        (w_map_b, pool_scale, w_pp_b, w_pa_b, w_out_b, w_up_b, w_conv, b_conv, w_down_b),
        (g_post_mix, g_pre_ffn, g_post_ffn))
    conv_state_s = jnp.concatenate([state_conv[0][:, 1:], hu_s[:, None]], axis=1)

    y_p, conv_tail = _back(x_prompt, mod_p, gp, sg, o_list, lse_list, w_pa_b, w_out_b, w_up_b, w_conv,
                           b_conv, w_down_b, g_post_mix, g_pre_ffn, g_post_ffn)

    outs = [y_p, y_s[:, None]]
    for g in range(N_GROUPS):
        for j in range(2):
            f = kv_t[2 * g + j]
            outs.append(f.reshape(bp, N_SLOTS, HEAD_DIM, f.shape[2]).transpose(0, 3, 1, 2)[None])
            outs.append(new_caches[2 * g + j])
    outs += [u_tail[None, :, HALO - POOL_HIST:], pool_state_t.transpose(1, 0, 2)[None],
             conv_tail[None, :, CONV_HALO - (CONV_W - 1):], conv_state_s[None]]
    return tuple(outs)
```

```python
import functools

import jax
import jax.numpy as jnp
from jax import lax
from jax.experimental import pallas as pl
from jax.experimental.pallas import tpu as pltpu

F32 = jnp.float32
BF16 = jnp.bfloat16

D_MODEL = 1024
EPS = 1e-6
POOL_WINDOWS = (2, 4, 8, 16)
POOL_GROUP = D_MODEL // len(POOL_WINDOWS)
POOL_HIST = max(POOL_WINDOWS) - 1
ATT_GROUPS = ((128, 1), (512, 4), (2048, 16))
SPAN = 128
N_GROUPS = len(ATT_GROUPS)
N_SLOTS = 8
HEAD_DIM = 64
D_GROUP = N_SLOTS * HEAD_DIM
D_ATT = N_GROUPS * D_GROUP
D_FF = 2816
CONV_W = 3
COL_U, COL_Q, COL_K, COL_V, COL_ZP, COL_ZA, D_IN = 0, 1024, 2560, 4096, 5632, 6656, 7680
Q_SCALE = HEAD_DIM ** -0.5
NEG = -1e30

LANES = 128
GROUP_SLABS = D_GROUP // LANES
HALO = 16
CONV_HALO = 8
TM = 256
FF_CHUNK = 1408
VMEM_LIMIT = 56 * 1024 * 1024
CACHE_BLOCK_BYTES = 4 * 1024 * 1024
N_SATTN_IN = 8


def _const_spec(shape):
    nd = len(shape)
    return pl.BlockSpec(shape, lambda *_: (0,) * nd, pipeline_mode=pl.Buffered(1))


def _rms(x, g):
    return x * lax.rsqrt(jnp.mean(x * x, axis=-1, keepdims=True) + EPS) * g


def _sigmoid(x):
    return 1.0 / (1.0 + jnp.exp(-x))


def _gelu_tanh(x):
    return 0.5 * x * (1.0 + jnp.tanh(0.7978845608028654 * (x + 0.044715 * (x * x * x))))


def _dot(a, b):
    return jnp.dot(a, b, preferred_element_type=F32)


def _params(*semantics):
    return pltpu.CompilerParams(dimension_semantics=semantics, vmem_limit_bytes=VMEM_LIMIT)


def _ada_kernel(c_ref, w_ref, b_ref, o_ref):
    c = c_ref[...]
    s = c * _sigmoid(c)
    o_ref[...] = _dot(s.astype(BF16), w_ref[...].astype(BF16)) + b_ref[...]


def _ada(c_all, w_ada, b_ada):
    m = c_all.shape[0]
    n = w_ada.shape[1]
    tn = 1536
    return pl.pallas_call(
        _ada_kernel,
        grid=(n // tn,),
        in_specs=[pl.BlockSpec((m, D_MODEL), lambda j: (0, 0)),
                  pl.BlockSpec((D_MODEL, tn), lambda j: (0, j)),
                  pl.BlockSpec((1, tn), lambda j: (0, j))],
        out_specs=pl.BlockSpec((m, tn), lambda j: (0, j)),
        out_shape=jax.ShapeDtypeStruct((m, n), F32),
        compiler_params=_params("arbitrary"),
        name="ada",
    )(c_all, w_ada, b_ada)


def _front_kernel(x_ref, mod_ref, g_ref, win_ref, wmap_ref, pscale_ref, wpp_ref,
                  gp_ref, sg_ref, q0, k0, v0, q1, k1, v1, q2, k2, v2,
                  kf0, vf0, kf1, vf1, kf2, vf2, utail_ref, ubuf, slabs):
    i = pl.program_id(1)
    tm = x_ref.shape[1]
    x = x_ref[0]
    shift = mod_ref[0, 0:1, :]
    scale = mod_ref[0, 1:2, :]
    h = (_rms(x, g_ref[...]) * (1.0 + scale) + shift).astype(BF16)

    u = _dot(h, win_ref[:, COL_U:COL_Q])

    @pl.when(i == 0)
    def _():
        ubuf[0:HALO, :] = jnp.zeros((HALO, D_MODEL), F32)

    @pl.when(i > 0)
    def _():
        ubuf[0:HALO, :] = ubuf[tm:tm + HALO, :]

    ubuf[HALO:HALO + tm, :] = u
    utail_ref[0] = u[tm - HALO:, :]

    pos = i * tm + lax.broadcasted_iota(jnp.int32, (tm, 1), 0)
    mixed = []
    for g, win in enumerate(POOL_WINDOWS):
        lo, hi = g * POOL_GROUP, (g + 1) * POOL_GROUP
        wsum = ubuf[HALO:HALO + tm, lo:hi]
        for j in range(1, win):
            wsum = wsum + ubuf[HALO - j:HALO - j + tm, lo:hi]
        count = jnp.minimum(pos + 1, win).astype(F32)
        pooled = wsum / count - ubuf[HALO:HALO + tm, lo:hi]
        mixed.append(_dot(pooled.astype(BF16), wmap_ref[g]))
    mixed = jnp.concatenate(mixed, axis=-1) * pscale_ref[...]
    y_pool = _dot(mixed.astype(BF16), wpp_ref[...])
    gp_ref[0] = (_sigmoid(_dot(h, win_ref[:, COL_ZP:COL_ZA])) * y_pool).astype(BF16)
    sg_ref[0] = _sigmoid(_dot(h, win_ref[:, COL_ZA:D_IN])).astype(BF16)

    def emit(val, refs, frefs):
        for j in range(D_ATT // LANES):
            slabs[j] = val[:, j * LANES:(j + 1) * LANES]
        for g, (_, dil) in enumerate(ATT_GROUPS):
            if dil == 1:
                refs[g][0, 0] = val[:, g * D_GROUP:(g + 1) * D_GROUP].astype(BF16)
                continue
            for r in range(dil):
                for jj in range(GROUP_SLABS):
                    piece = slabs[g * GROUP_SLABS + jj, pl.ds(r, tm // dil, stride=dil), :]
                    refs[g][0, r, :, jj * LANES:(jj + 1) * LANES] = piece.astype(BF16)
        for g, fref in enumerate(frefs):
            rows = fref.shape[2]
            fref[0] = val[tm - rows:, g * D_GROUP:(g + 1) * D_GROUP].T

    emit(_dot(h, win_ref[:, COL_Q:COL_K]) * Q_SCALE, (q0, q1, q2), ())
    emit(_dot(h, win_ref[:, COL_K:COL_V]), (k0, k1, k2), (kf0, kf1, kf2))
    emit(_dot(h, win_ref[:, COL_V:COL_ZP]), (v0, v1, v2), (vf0, vf1, vf2))


def _front(x, mod, g_pre, w_in, w_map, pool_scale, w_pp):
    b, t, d = x.shape
    tm = TM
    nblk = t // tm
    row = lambda shape: pl.BlockSpec(shape, lambda bi, i: (bi, i, 0))
    in_specs = [row((1, tm, d)),
                pl.BlockSpec((1, 6, d), lambda bi, i: (bi, 0, 0)),
                _const_spec((1, d)), _const_spec(w_in.shape), _const_spec(w_map.shape),
                _const_spec((1, d)), _const_spec(w_pp.shape)]
    out_shape = [jax.ShapeDtypeStruct((b, t, d), BF16), jax.ShapeDtypeStruct((b, t, d), BF16)]
    out_specs = [row((1, tm, d)), row((1, tm, d))]
    for _, dil in ATT_GROUPS:
        for _ in range(3):
            out_shape.append(jax.ShapeDtypeStruct((b, dil, t // dil, D_GROUP), BF16))
            out_specs.append(pl.BlockSpec((1, dil, tm // dil, D_GROUP), lambda bi, i: (bi, 0, i, 0)))
    for win, _ in ATT_GROUPS:
        keep = min(win, t)
        rows = min(keep, tm)
        first = nblk - keep // rows
        for _ in range(2):
            out_shape.append(jax.ShapeDtypeStruct((b, D_GROUP, keep), F32))
            out_specs.append(pl.BlockSpec(
                (1, D_GROUP, rows), lambda bi, i, first=first: (bi, 0, jnp.maximum(i - first, 0))))
    out_shape.append(jax.ShapeDtypeStruct((b, HALO, d), F32))
    out_specs.append(pl.BlockSpec((1, HALO, d), lambda bi, i: (bi, 0, 0)))
    return pl.pallas_call(
        _front_kernel,
        grid=(b, nblk),
        in_specs=in_specs, out_specs=out_specs, out_shape=out_shape,
        scratch_shapes=[pltpu.VMEM((HALO + tm, d), F32),
                        pltpu.VMEM((D_ATT // LANES, tm, LANES), F32)],
        compiler_params=_params("arbitrary", "arbitrary"),
        name="front",
    )(x, mod, g_pre, w_in, w_map, pool_scale, w_pp)


def _attn_unit_tasks(q_ref, kp_ref, kc_ref, vp_ref, vc_ref, o_ref, lse_ref, n, r, dil):
    s2 = 2 * SPAN
    lane = lax.broadcasted_iota(jnp.int32, (SPAN, LANES), 1)
    low = lane < HEAD_DIM
    rows = pl.ds(r, SPAN, stride=dil) if dil > 1 else slice(None)
    state = {"lse": jnp.zeros((SPAN, LANES), F32)}

    def pair(j):
        row = lax.broadcasted_iota(jnp.int32, (s2, s2), 0) % SPAN
        col = lax.broadcasted_iota(jnp.int32, (s2, s2), 1)
        first_row = row + jnp.where(n > 0, 0, s2)
        mask = jnp.where(col < SPAN, col - first_row, row - (col - SPAN)) >= 0
        sl = slice(j * LANES, (j + 1) * LANES)
        q2 = q_ref[:, sl]
        zero = jnp.zeros_like(q2)
        qs = jnp.concatenate([jnp.where(low, q2, zero), jnp.where(low, zero, q2)], axis=0)
        kk = jnp.concatenate([kp_ref[:, sl], kc_ref[:, sl]], axis=0)
        vv = jnp.concatenate([vp_ref[:, sl], vc_ref[:, sl]], axis=0)
        s = lax.dot_general(qs, kk, (((1,), (1,)), ((), ())), preferred_element_type=F32)
        s = jnp.where(mask, s, NEG)
        m = jnp.max(s, axis=-1, keepdims=True)
        p = jnp.exp(s - m)
        den = jnp.sum(p, axis=-1, keepdims=True)
        o2 = _dot(p.astype(BF16), vv) / den
        lse2 = m + jnp.log(den)
        o_ref[j, rows, :] = jnp.where(low, o2[:SPAN], o2[SPAN:])
        tile = jnp.where(lane == 2 * j, lse2[:SPAN], state["lse"])
        state["lse"] = jnp.where(lane == 2 * j + 1, lse2[SPAN:], tile)

    def finish():
        lse_ref[rows, :] = state["lse"]

    return [functools.partial(pair, j) for j in range(N_SLOTS // 2)] + [finish]


def _attn_unit_specs(q, dil):
    b, _, n_sub, _ = q.shape
    nb = n_sub // SPAN
    t = n_sub * dil
    unit = lambda i: (i // (nb * dil), (i // dil) % nb, i % dil)

    def cur(i):
        bi, n, r = unit(i)
        return bi, r, n, 0

    def prev(i):
        bi, n, r = unit(i)
        return bi, r, jnp.maximum(n - 1, 0), 0

    blk = lambda f: pl.BlockSpec((None, None, SPAN, D_GROUP), f)
    in_specs = [blk(cur), blk(prev), blk(cur), blk(prev), blk(cur)]
    out_specs = [pl.BlockSpec((None, GROUP_SLABS, SPAN * dil, LANES), lambda i: (unit(i)[0], 0, unit(i)[1], 0)),
                 pl.BlockSpec((None, SPAN * dil, LANES), lambda i: (unit(i)[0], unit(i)[1], 0))]
    out_shape = [jax.ShapeDtypeStruct((b, GROUP_SLABS, t, LANES), F32),
                 jax.ShapeDtypeStruct((b, t, LANES), F32)]
    return b * nb * dil, nb, in_specs, out_specs, out_shape


def _merge_groups(o_list, lse_list):
    r = lax.broadcasted_iota(jnp.int32, (LANES, D_GROUP), 0)
    c = lax.broadcasted_iota(jnp.int32, (LANES, D_GROUP), 1)
    expand = jnp.where(c // HEAD_DIM == r, 1.0, 0.0).astype(BF16)
    top = jnp.maximum(jnp.maximum(lse_list[0], lse_list[1]), lse_list[2])
    e = [jnp.exp(l - top) for l in lse_list]
    den = e[0] + e[1] + e[2]
    out = None
    for eg, og in zip(e, o_list):
        w = eg / den
        hi = w.astype(BF16)
        lo = (w - hi.astype(F32)).astype(BF16)
        term = (_dot(hi, expand) + _dot(lo, expand)) * og
        out = term if out is None else out + term
    return out


def _gated_ffn(conv_taps, wconv_ref, bconv_ref, wdown_ref, rows):
    acc = jnp.zeros((rows, D_MODEL), F32)
    for c in range(D_FF // FF_CHUNK):
        halves = []
        for base in (0, D_FF):
            cs = slice(base + c * FF_CHUNK, base + (c + 1) * FF_CHUNK)
            hc = bconv_ref[:, cs]
            for j in range(CONV_W):
                hc = hc + conv_taps(j, cs) * wconv_ref[j:j + 1, cs]
            halves.append(hc)
        gated = (_gelu_tanh(halves[0]) * halves[1]).astype(BF16)
        acc = acc + _dot(gated, wdown_ref[c * FF_CHUNK:(c + 1) * FF_CHUNK, :])
    return acc


def _back_kernel(x_ref, mod_ref, gp_ref, sg_ref, o0, o1, o2, l0, l1, l2,
                 wpa_ref, wout_ref, wup_ref, wconv_ref, bconv_ref, wdown_ref,
                 gpost_ref, gpre_ref, gffn_ref, y_ref, ctail_ref, hbuf):
    i = pl.program_id(1)
    tm = x_ref.shape[1]
    mod = lambda r: mod_ref[0, r:r + 1, :]
    o_list = [jnp.concatenate([ref[j] for j in range(GROUP_SLABS)], axis=-1) for ref in (o0, o1, o2)]
    o = _merge_groups(o_list, [l0[0], l1[0], l2[0]])
    y_att = _dot(o.astype(BF16), wpa_ref[...])
    mix = gp_ref[0].astype(F32) + sg_ref[0].astype(F32) * y_att
    x1 = x_ref[0] + mod(2) * _rms(_dot(mix.astype(BF16), wout_ref[...]), gpost_ref[...])
    h2 = (_rms(x1, gpre_ref[...]) * (1.0 + mod(4)) + mod(3)).astype(BF16)

    @pl.when(i == 0)
    def _():
        hbuf[0:CONV_HALO, :] = jnp.zeros((CONV_HALO, 2 * D_FF), F32)

    @pl.when(i > 0)
    def _():
        hbuf[0:CONV_HALO, :] = hbuf[tm:tm + CONV_HALO, :]

    for c in range(2 * D_FF // FF_CHUNK):
        cs = slice(c * FF_CHUNK, (c + 1) * FF_CHUNK)
        hbuf[CONV_HALO:CONV_HALO + tm, cs] = _dot(h2, wup_ref[:, cs])

    def taps(j, cs):
        off = CONV_HALO - (CONV_W - 1) + j
        return hbuf[off:off + tm, cs]

    ffn = _gated_ffn(taps, wconv_ref, bconv_ref, wdown_ref, tm)
    y_ref[0] = x1 + mod(5) * _rms(ffn, gffn_ref[...])
    ctail_ref[0] = hbuf[tm:tm + CONV_HALO, :]


def _back(x, mod, gp, sg, o_list, lse_list, w_pa, w_out, w_up, w_conv, b_conv, w_down,
          g_post, g_pre, g_ffn):
    b, t, d = x.shape
    tm = TM
    row = lambda shape: pl.BlockSpec(shape, lambda bi, i: (bi, i, 0))
    slab = pl.BlockSpec((None, GROUP_SLABS, tm, LANES), lambda bi, i: (bi, 0, i, 0))
    in_specs = ([row((1, tm, d)), pl.BlockSpec((1, 6, d), lambda bi, i: (bi, 0, 0)),
                 row((1, tm, d)), row((1, tm, d))]
                + [slab] * N_GROUPS + [row((1, tm, LANES))] * N_GROUPS
                + [_const_spec(w.shape) for w in (w_pa, w_out, w_up, w_conv, b_conv, w_down)]
                + [_const_spec((1, d))] * 3)
    return pl.pallas_call(
        _back_kernel,
        grid=(b, t // tm),
        in_specs=in_specs,
        out_specs=[row((1, tm, d)), pl.BlockSpec((1, CONV_HALO, 2 * D_FF), lambda bi, i: (bi, 0, 0))],
        out_shape=[jax.ShapeDtypeStruct((b, t, d), F32),
                   jax.ShapeDtypeStruct((b, CONV_HALO, 2 * D_FF), F32)],
        scratch_shapes=[pltpu.VMEM((CONV_HALO + tm, 2 * D_FF), F32)],
        compiler_params=_params("arbitrary", "arbitrary"),
        name="back",
    )(x, mod, gp, sg, *o_list, *lse_list, w_pa, w_out, w_up, w_conv, b_conv, w_down,
      g_post, g_pre, g_ffn)


def _sfront_kernel(x_ref, shift_ref, scale_ref, g_ref, w_ref, o_ref, h_scr):
    @pl.when(pl.program_id(0) == 0)
    def _():
        h = _rms(x_ref[...], g_ref[...]) * (1.0 + scale_ref[...]) + shift_ref[...]
        h_scr[...] = h.astype(BF16)

    o_ref[...] = _dot(h_scr[...], w_ref[...])


def _sfront(x, mod, g_pre, w_in):
    m, d = x.shape
    n = w_in.shape[1]
    tn = 1536
    return pl.pallas_call(
        _sfront_kernel,
        grid=(n // tn,),
        in_specs=[pl.BlockSpec((m, d), lambda j: (0, 0)),
                  pl.BlockSpec((m, d), lambda j: (0, 0)),
                  pl.BlockSpec((m, d), lambda j: (0, 1)),
                  pl.BlockSpec((1, d), lambda j: (0, 0)),
                  pl.BlockSpec((d, tn), lambda j: (0, j))],
        out_specs=pl.BlockSpec((m, tn), lambda j: (0, j)),
        out_shape=jax.ShapeDtypeStruct((m, n), F32),
        scratch_shapes=[pltpu.VMEM((m, d), BF16)],
        compiler_params=_params("arbitrary"),
        name="sfront",
    )(x, mod, mod, g_pre, w_in)


def _sattn_kernel(*refs, dil, units):
    qrow_ref, q_ref, kn_ref, vn_ref, knt_ref, vnt_ref, kt_ref, vt_ref = refs[:N_SATTN_IN]
    n_in = N_SATTN_IN + 5 * len(units)
    o_ref, lse_ref, kto_ref, vto_ref = refs[n_in:n_in + 4]
    step = pl.program_id(0)
    side_tasks = []
    for g, (nb, unit_dil) in enumerate(units):
        ins = refs[N_SATTN_IN + 5 * g:N_SATTN_IN + 5 * g + 5]
        side_tasks += _attn_unit_tasks(*ins, *refs[n_in + 4 + 2 * g:n_in + 6 + 2 * g],
                                       (step // unit_dil) % nb, step % unit_dil, unit_dil)

    bb, n_slots, dh, length = kt_ref.shape
    lane = lax.broadcasted_iota(jnp.int32, (1, length), 1)
    attended = (lane & (dil - 1)) == 0
    last = lane == length - 1
    slot = lax.broadcasted_iota(jnp.int32, (n_slots, 1), 0)
    state = [dict(sc=jnp.zeros((n_slots, length), F32), o=jnp.zeros((n_slots, dh), F32)) for _ in range(bb)]

    def score_slot(b, s):
        kt = kt_ref[b, s]
        state[b]["sc"] = state[b]["sc"] + _dot(qrow_ref[b, s], kt.astype(BF16))
        kto_ref[b, s] = jnp.where(last, knt_ref[b, :, s:s + 1], pltpu.roll(kt, length - 1, 1))

    def softmax(b):
        sc_new = jnp.sum(kn_ref[b] * q_ref[b], axis=-1, keepdims=True) * Q_SCALE
        sc = jnp.where(attended, state[b]["sc"], NEG)
        m = jnp.maximum(jnp.max(sc, axis=-1, keepdims=True), sc_new)
        p = jnp.exp(sc - m)
        p_new = jnp.exp(sc_new - m)
        den = jnp.sum(p, axis=-1, keepdims=True) + p_new
        lse_ref[b] = m + jnp.log(den)
        state[b].update(p=p.astype(BF16), p_new=p_new, inv=1.0 / den)

    def value_slot(b, s):
        vt = vt_ref[b, s]
        pv = lax.dot_general(state[b]["p"], vt.astype(BF16), (((1,), (1,)), ((), ())),
                             preferred_element_type=F32)
        state[b]["o"] = jnp.where(slot == s, pv, state[b]["o"])
        vto_ref[b, s] = jnp.where(last, vnt_ref[b, :, s:s + 1], pltpu.roll(vt, length - 1, 1))

    def finish(b):
        o_ref[b] = (state[b]["o"] + vn_ref[b] * state[b]["p_new"]) * state[b]["inv"]

    for b in range(bb):
        for s in range(n_slots):
            score_slot(b, s)
        softmax(b)
        for s in range(n_slots):
            value_slot(b, s)
        finish(b)
    for task in side_tasks:
        task()


def _sattn(q, kn, vn, kt, vt, dil, prompt=()):
    b, n_slots, dh, length = kt.shape
    bb = max(1, CACHE_BLOCK_BYTES // (n_slots * dh * length * 4))
    steps = b // bb
    own_row = jnp.eye(n_slots, dtype=F32)[None, :, :, None]
    qrow = (own_row * (q * Q_SCALE)[:, :, None, :]).astype(BF16)
    new = pl.BlockSpec((bb, n_slots, dh), lambda i: (i, 0, 0))
    column = pl.BlockSpec((bb, dh, n_slots), lambda i: (i, 0, 0))
    cache = pl.BlockSpec((bb, n_slots, dh, length), lambda i: (i, 0, 0, 0))
    in_specs = [pl.BlockSpec((bb, n_slots, n_slots, dh), lambda i: (i, 0, 0, 0)),
                new, new, new, column, column, cache, cache]
    assert len(in_specs) == N_SATTN_IN
    out_specs = [new, pl.BlockSpec((bb, n_slots, 1), lambda i: (i, 0, 0)), cache, cache]
    out_shape = [jax.ShapeDtypeStruct((b, n_slots, dh), F32),
                 jax.ShapeDtypeStruct((b, n_slots, 1), F32),
                 jax.ShapeDtypeStruct(kt.shape, F32), jax.ShapeDtypeStruct(vt.shape, F32)]
    args = [qrow, q, kn, vn, kn.transpose(0, 2, 1), vn.transpose(0, 2, 1), kt, vt]
    units = []
    for pq, pk, pv, pdil in prompt:
        n_units, nb, ins, outs, shapes = _attn_unit_specs(pq, pdil)
        assert n_units == steps, (n_units, steps)
        units.append((nb, pdil))
        in_specs += ins
        out_specs += outs
        out_shape += shapes
        args += [pq, pk, pk, pv, pv]
    return pl.pallas_call(
        functools.partial(_sattn_kernel, dil=dil, units=tuple(units)),
        grid=(steps,),
        in_specs=in_specs, out_specs=out_specs, out_shape=out_shape,
        compiler_params=_params("arbitrary"),
        name="sattn",
    )(*args)


def _pool_state_kernel(hist_ref, u_ref, o_ref):
    newest = pl.program_id(0) == POOL_HIST - 1
    o_ref[0] = jnp.where(newest, u_ref[...], hist_ref[0])


def _pool_state(hist_t, proj):
    steps, m, d = hist_t.shape
    return pl.pallas_call(
        _pool_state_kernel,
        grid=(steps,),
        in_specs=[pl.BlockSpec((1, m, d), lambda j: (jnp.minimum(j + 1, steps - 1), 0, 0)),
                  pl.BlockSpec((m, d), lambda j: (0, COL_U // d))],
        out_specs=pl.BlockSpec((1, m, d), lambda j: (j, 0, 0)),
        out_shape=jax.ShapeDtypeStruct(hist_t.shape, F32),
        compiler_params=_params("arbitrary"),
        name="pool_state",
    )(hist_t, proj)


def _sback_kernel(x_ref, mod_ref, proj_ref, hist_ref, o0, o1, o2, l0, l1, l2, chist_ref,
                  wmap_ref, pscale_ref, wpp_ref, wpa_ref, wout_ref, wup_ref, wconv_ref, bconv_ref,
                  wdown_ref, gpost_ref, gpre_ref, gffn_ref, y_ref, hu_ref):
    m = x_ref.shape[0]
    mod = lambda r: mod_ref[:, r * D_MODEL:(r + 1) * D_MODEL]
    u = proj_ref[:, COL_U:COL_Q]
    mixed = []
    for g, win in enumerate(POOL_WINDOWS):
        lo, hi = g * POOL_GROUP, (g + 1) * POOL_GROUP
        wsum = u[:, lo:hi]
        for j in range(1, win):
            wsum = wsum + hist_ref[POOL_HIST - j, :, lo:hi]
        pooled = wsum / float(win) - u[:, lo:hi]
        mixed.append(_dot(pooled.astype(BF16), wmap_ref[g]))
    mixed = jnp.concatenate(mixed, axis=-1) * pscale_ref[...]
    y_pool = _dot(mixed.astype(BF16), wpp_ref[...])
    o = _merge_groups([o0[...], o1[...], o2[...]], [l0[...], l1[...], l2[...]])
    y_att = _dot(o.astype(BF16), wpa_ref[...])
    mix = (_sigmoid(proj_ref[:, COL_ZP:COL_ZA]) * y_pool + _sigmoid(proj_ref[:, COL_ZA:D_IN]) * y_att)
    x1 = x_ref[...] + mod(2) * _rms(_dot(mix.astype(BF16), wout_ref[...]), gpost_ref[...])
    h2 = (_rms(x1, gpre_ref[...]) * (1.0 + mod(4)) + mod(3)).astype(BF16)

    for c in range(2 * D_FF // FF_CHUNK):
        cs = slice(c * FF_CHUNK, (c + 1) * FF_CHUNK)
        hu_ref[:, cs] = _dot(h2, wup_ref[:, cs])

    def taps(j, cs):
        return hu_ref[:, cs] if j == CONV_W - 1 else chist_ref[j, :, cs]

    ffn = _gated_ffn(taps, wconv_ref, bconv_ref, wdown_ref, m)
    y_ref[...] = x1 + mod(5) * _rms(ffn, gffn_ref[...])


def _sback(x, mod, proj, hist_t, o_list, lse_list, chist_t, weights, gains):
    m, d = x.shape
    args = [x, mod, proj, hist_t, *o_list, *lse_list, chist_t, *weights, *gains]
    full = lambda shape: pl.BlockSpec(shape, lambda i: (0,) * len(shape))
    out_shapes = [(m, d), (m, 2 * D_FF)]
    return pl.pallas_call(
        _sback_kernel,
        grid=(1,),
        in_specs=[_const_spec(a.shape) for a in args],
        out_specs=[full(s) for s in out_shapes],
        out_shape=[jax.ShapeDtypeStruct(s, F32) for s in out_shapes],
        compiler_params=_params("arbitrary"),
        name="sback",
    )(*args)


def kernel(x_prompt, x_sample, c_prompt, c_sample, cache_k_w128, cache_v_w128, cache_k_w512, cache_v_w512,
           cache_k_w2048, cache_v_w2048, state_pool, state_conv, w_ada, b_ada, g_pre_mix, g_post_mix,
           g_pre_ffn, g_post_ffn, w_in, w_pool_map, pool_scale, w_proj_pool, w_proj_att, w_out, w_up,
           w_conv, b_conv, w_down):
    bp, tp, d = x_prompt.shape
    bs = x_sample.shape[0]
    (w_ada, b_ada, g_pre_mix, g_post_mix, g_pre_ffn, g_post_ffn, w_in, w_pool_map, pool_scale,
     w_proj_pool, w_proj_att, w_out, w_up, w_conv, b_conv, w_down) = (
        w[0] for w in (w_ada, b_ada, g_pre_mix, g_post_mix, g_pre_ffn, g_post_ffn, w_in, w_pool_map,
                       pool_scale, w_proj_pool, w_proj_att, w_out, w_up, w_conv, b_conv, w_down))
    caches = [(cache_k_w128[0], cache_v_w128[0]), (cache_k_w512[0], cache_v_w512[0]),
              (cache_k_w2048[0], cache_v_w2048[0])]
    g_pre_mix, g_post_mix, g_pre_ffn, g_post_ffn, pool_scale, b_conv, b_ada = (
        a.reshape(1, -1) for a in (g_pre_mix, g_post_mix, g_pre_ffn, g_post_ffn, pool_scale, b_conv, b_ada))
    w_in_b, w_map_b, w_pp_b, w_pa_b, w_out_b, w_up_b, w_down_b = (
        w.astype(BF16) for w in (w_in, w_pool_map, w_proj_pool, w_proj_att, w_out, w_up, w_down))

    mod = _ada(jnp.concatenate([c_prompt, c_sample], axis=0), w_ada, b_ada)
    mod_p = mod[:bp].reshape(bp, 6, d)
    mod_s = mod[bp:]

    front = _front(x_prompt, mod_p, g_pre_mix, w_in_b, w_map_b, pool_scale, w_pp_b)
    gp, sg = front[0], front[1]
    qkv = front[2:11]
    kv_t = front[11:17]
    u_tail = front[17]
    prompt_units = [(qkv[3 * g], qkv[3 * g + 1], qkv[3 * g + 2], dil) for g, (_, dil) in enumerate(ATT_GROUPS)]

    proj_s = _sfront(x_sample[:, 0], mod_s, g_pre_mix, w_in_b)
    heads = lambda lo: proj_s[:, lo:lo + D_ATT].reshape(bs, N_GROUPS, N_SLOTS, HEAD_DIM).transpose(1, 0, 2, 3)
    q_s, k_s, v_s = heads(COL_Q), heads(COL_K), heads(COL_V)
    o_s, lse_s, new_caches = [], [], []
    for g, ((_, dil), (kc, vc)) in enumerate(zip(ATT_GROUPS, caches)):
        kt, vt = kc.transpose(0, 2, 3, 1), vc.transpose(0, 2, 3, 1)
        if g == N_GROUPS - 1:
            res = _sattn(q_s[g], k_s[g], v_s[g], kt, vt, dil, prompt_units)
            o_list, lse_list = list(res[4::2]), list(res[5::2])
        else:
            res = _sattn(q_s[g], k_s[g], v_s[g], kt, vt, dil)
        o_g, lse_g, kt_new, vt_new = res[:4]
        o_s.append(o_g.reshape(bs, D_GROUP))
        lse_s.append(jnp.pad(lse_g[:, :, 0], ((0, 0), (0, LANES - N_SLOTS))))
        new_caches += [kt_new.transpose(0, 3, 1, 2)[None], vt_new.transpose(0, 3, 1, 2)[None]]
    hist_t = state_pool[0].transpose(1, 0, 2)
    chist_t = state_conv[0].transpose(1, 0, 2)
    pool_state_t = _pool_state(hist_t, proj_s)
    y_s, hu_s = _sback(
        x_sample[:, 0], mod_s, proj_s, hist_t, o_s, lse_s, chist_t,
        (w_map_b, pool_scale, w_pp_b, w_pa_b, w_out_b, w_up_b, w_conv, b_conv, w_down_b),
        (g_post_mix, g_pre_ffn, g_post_ffn))
    conv_state_s = jnp.concatenate([state_conv[0][:, 1:], hu_s[:, None]], axis=1)

    y_p, conv_tail = _back(x_prompt, mod_p, gp, sg, o_list, lse_list, w_pa_b, w_out_b, w_up_b, w_conv,
                           b_conv, w_down_b, g_post_mix, g_pre_ffn, g_post_ffn)

    outs = [y_p, y_s[:, None]]
    for g in range(N_GROUPS):
        for j in range(2):
            f = kv_t[2 * g + j]
            outs.append(f.reshape(bp, N_SLOTS, HEAD_DIM, f.shape[2]).transpose(0, 3, 1, 2)[None])
            outs.append(new_caches[2 * g + j])
    outs += [u_tail[None, :, HALO - POOL_HIST:], pool_state_t.transpose(1, 0, 2)[None],
             conv_tail[None, :, CONV_HALO - (CONV_W - 1):], conv_state_s[None]]
    return tuple(outs)
```

```python
import functools

import jax
import jax.numpy as jnp
from jax import lax
from jax.experimental import pallas as pl
from jax.experimental.pallas import tpu as pltpu

F32 = jnp.float32
BF16 = jnp.bfloat16

D_MODEL = 1024
EPS = 1e-6
POOL_WINDOWS = (2, 4, 8, 16)
POOL_GROUP = D_MODEL // len(POOL_WINDOWS)
POOL_HIST = max(POOL_WINDOWS) - 1
ATT_GROUPS = ((128, 1), (512, 4), (2048, 16))
SPAN = 128
N_GROUPS = len(ATT_GROUPS)
N_SLOTS = 8
HEAD_DIM = 64
D_GROUP = N_SLOTS * HEAD_DIM
D_ATT = N_GROUPS * D_GROUP
D_FF = 2816
CONV_W = 3
COL_U, COL_Q, COL_K, COL_V, COL_ZP, COL_ZA, D_IN = 0, 1024, 2560, 4096, 5632, 6656, 7680
Q_SCALE = HEAD_DIM ** -0.5
NEG = -1e30

LANES = 128
GROUP_SLABS = D_GROUP // LANES
HALO = 16
CONV_HALO = 8
TM = 256
TM_BACK = 512
FFN_ROWS = 512
VMEM_LIMIT = 62 * 1024 * 1024
CACHE_BLOCK_BYTES = 4 * 1024 * 1024
N_SATTN_IN = 8


def _const_spec(shape):
    nd = len(shape)
    return pl.BlockSpec(shape, lambda *_: (0,) * nd, pipeline_mode=pl.Buffered(1))


def _rms(x, g):
    return x * lax.rsqrt(jnp.mean(x * x, axis=-1, keepdims=True) + EPS) * g


def _sigmoid(x):
    return 1.0 / (1.0 + jnp.exp(-x))


def _gelu_tanh(x):
    return 0.5 * x * (1.0 + jnp.tanh(0.7978845608028654 * (x + 0.044715 * (x * x * x))))


def _dot(a, b):
    return jnp.dot(a, b, preferred_element_type=F32)


def _params(*semantics):
    return pltpu.CompilerParams(dimension_semantics=semantics, vmem_limit_bytes=VMEM_LIMIT)


def _ada_kernel(c_ref, w_ref, b_ref, o_ref):
    c = c_ref[...]
    s = c * _sigmoid(c)
    o_ref[...] = _dot(s.astype(BF16), w_ref[...].astype(BF16)) + b_ref[...]


def _ada(c_all, w_ada, b_ada):
    m = c_all.shape[0]
    n = w_ada.shape[1]
    tn = 1536
    return pl.pallas_call(
        _ada_kernel,
        grid=(n // tn,),
        in_specs=[pl.BlockSpec((m, D_MODEL), lambda j: (0, 0)),
                  pl.BlockSpec((D_MODEL, tn), lambda j: (0, j)),
                  pl.BlockSpec((1, tn), lambda j: (0, j))],
        out_specs=pl.BlockSpec((m, tn), lambda j: (0, j)),
        out_shape=jax.ShapeDtypeStruct((m, n), F32),
        compiler_params=_params("arbitrary"),
        name="ada",
    )(c_all, w_ada, b_ada)


def _front_kernel(x_ref, mod_ref, g_ref, win_ref, wmap_ref, pscale_ref, wpp_ref,
                  gp_ref, sg_ref, q0, k0, v0, q1, k1, v1, q2, k2, v2,
                  kf0, vf0, kf1, vf1, kf2, vf2, utail_ref, ubuf, slabs):
    i = pl.program_id(1)
    tm = x_ref.shape[1]

    @pl.when((pl.program_id(0) == 0) & (i == 0))
    def _():
        ubuf[tm:tm + HALO, :] = jnp.zeros((HALO, D_MODEL), F32)

    x = x_ref[0]
    shift = mod_ref[0, 0:1, :]
    scale = mod_ref[0, 1:2, :]
    h = (_rms(x, g_ref[...]) * (1.0 + scale) + shift).astype(BF16)

    u = _dot(h, win_ref[:, COL_U:COL_Q])
    ubuf[0:HALO, :] = jnp.where(i > 0, ubuf[tm:tm + HALO, :], 0.0)
    ubuf[HALO:HALO + tm, :] = u
    utail_ref[0] = u[tm - HALO:, :]

    pos = i * tm + lax.broadcasted_iota(jnp.int32, (tm, 1), 0)
    pooled = []
    for g, win in enumerate(POOL_WINDOWS):
        lo, hi = g * POOL_GROUP, (g + 1) * POOL_GROUP
        wsum = ubuf[HALO:HALO + tm, lo:hi]
        for j in range(1, win):
            wsum = wsum + ubuf[HALO - j:HALO - j + tm, lo:hi]
        count = jnp.minimum(pos + 1, win).astype(F32)
        pooled.append((wsum / count - ubuf[HALO:HALO + tm, lo:hi]).astype(BF16))

    sg_ref[0] = _sigmoid(_dot(h, win_ref[:, COL_ZA:D_IN])).astype(BF16)
    pool_gate = _sigmoid(_dot(h, win_ref[:, COL_ZP:COL_ZA]))

    def emit(val, refs, frefs):
        for j in range(D_ATT // LANES):
            slabs[j] = val[:, j * LANES:(j + 1) * LANES]
        for g, (_, dil) in enumerate(ATT_GROUPS):
            if dil == 1:
                refs[g][0, 0] = val[:, g * D_GROUP:(g + 1) * D_GROUP].astype(BF16)
                continue
            for r in range(dil):
                for jj in range(GROUP_SLABS):
                    piece = slabs[g * GROUP_SLABS + jj, pl.ds(r, tm // dil, stride=dil), :]
                    refs[g][0, r, :, jj * LANES:(jj + 1) * LANES] = piece.astype(BF16)
        for g, fref in enumerate(frefs):
            rows = fref.shape[2]
            fref[0] = val[tm - rows:, g * D_GROUP:(g + 1) * D_GROUP].T

    emit(_dot(h, win_ref[:, COL_Q:COL_K]) * Q_SCALE, (q0, q1, q2), ())
    emit(_dot(h, win_ref[:, COL_K:COL_V]), (k0, k1, k2), (kf0, kf1, kf2))
    emit(_dot(h, win_ref[:, COL_V:COL_ZP]), (v0, v1, v2), (vf0, vf1, vf2))

    mixed = [_dot(p, wmap_ref[g]) for g, p in enumerate(pooled)]
    mixed = jnp.concatenate(mixed, axis=-1) * pscale_ref[...]
    y_pool = _dot(mixed.astype(BF16), wpp_ref[...])
    gp_ref[0] = (pool_gate * y_pool).astype(BF16)


def _front(x, mod, g_pre, w_in, w_map, pool_scale, w_pp):
    b, t, d = x.shape
    tm = TM
    nblk = t // tm
    row = lambda shape: pl.BlockSpec(shape, lambda bi, i: (bi, i, 0))
    in_specs = [row((1, tm, d)),
                pl.BlockSpec((1, 6, d), lambda bi, i: (bi, 0, 0)),
                _const_spec((1, d)), _const_spec(w_in.shape), _const_spec(w_map.shape),
                _const_spec((1, d)), _const_spec(w_pp.shape)]
    out_shape = [jax.ShapeDtypeStruct((b, t, d), BF16), jax.ShapeDtypeStruct((b, t, d), BF16)]
    out_specs = [row((1, tm, d)), row((1, tm, d))]
    for _, dil in ATT_GROUPS:
        for _ in range(3):
            out_shape.append(jax.ShapeDtypeStruct((b, dil, t // dil, D_GROUP), BF16))
            out_specs.append(pl.BlockSpec((1, dil, tm // dil, D_GROUP), lambda bi, i: (bi, 0, i, 0)))
    for win, _ in ATT_GROUPS:
        keep = min(win, t)
        rows = min(keep, tm)
        first = nblk - keep // rows
        for _ in range(2):
            out_shape.append(jax.ShapeDtypeStruct((b, D_GROUP, keep), F32))
            out_specs.append(pl.BlockSpec(
                (1, D_GROUP, rows), lambda bi, i, first=first: (bi, 0, jnp.maximum(i - first, 0))))
    out_shape.append(jax.ShapeDtypeStruct((b, HALO, d), F32))
    out_specs.append(pl.BlockSpec((1, HALO, d), lambda bi, i: (bi, 0, 0)))
    return pl.pallas_call(
        _front_kernel,
        grid=(b, nblk),
        in_specs=in_specs, out_specs=out_specs, out_shape=out_shape,
        scratch_shapes=[pltpu.VMEM((HALO + tm, d), F32),
                        pltpu.VMEM((D_ATT // LANES, tm, LANES), F32)],
        compiler_params=_params("arbitrary", "arbitrary"),
        name="front",
    )(x, mod, g_pre, w_in, w_map, pool_scale, w_pp)


def _attn_unit_tasks(q_ref, kp_ref, kc_ref, vp_ref, vc_ref, o_ref, lse_ref, n, r, dil):
    s2 = 2 * SPAN
    lane = lax.broadcasted_iota(jnp.int32, (SPAN, LANES), 1)
    low = lane < HEAD_DIM
    rows = pl.ds(r, SPAN, stride=dil) if dil > 1 else slice(None)
    state = {"lse": jnp.zeros((SPAN, LANES), F32)}

    def pair(j):
        row = lax.broadcasted_iota(jnp.int32, (s2, s2), 0) % SPAN
        col = lax.broadcasted_iota(jnp.int32, (s2, s2), 1)
        first_row = row + jnp.where(n > 0, 0, s2)
        mask = jnp.where(col < SPAN, col - first_row, row - (col - SPAN)) >= 0
        sl = slice(j * LANES, (j + 1) * LANES)
        q2 = q_ref[:, sl]
        zero = jnp.zeros_like(q2)
        qs = jnp.concatenate([jnp.where(low, q2, zero), jnp.where(low, zero, q2)], axis=0)
        kk = jnp.concatenate([kp_ref[:, sl], kc_ref[:, sl]], axis=0)
        vv = jnp.concatenate([vp_ref[:, sl], vc_ref[:, sl]], axis=0)
        s = lax.dot_general(qs, kk, (((1,), (1,)), ((), ())), preferred_element_type=F32)
        s = jnp.where(mask, s, NEG)
        m = jnp.max(s, axis=-1, keepdims=True)
        p = jnp.exp(s - m)
        den = jnp.sum(p, axis=-1, keepdims=True)
        o2 = _dot(p.astype(BF16), vv) / den
        lse2 = m + jnp.log(den)
        o_ref[j, rows, :] = jnp.where(low, o2[:SPAN], o2[SPAN:])
        tile = jnp.where(lane == 2 * j, lse2[:SPAN], state["lse"])
        state["lse"] = jnp.where(lane == 2 * j + 1, lse2[SPAN:], tile)

    def finish():
        lse_ref[rows, :] = state["lse"]

    return [functools.partial(pair, j) for j in range(N_SLOTS // 2)] + [finish]


def _attn_unit_specs(q, dil):
    b, _, n_sub, _ = q.shape
    nb = n_sub // SPAN
    t = n_sub * dil
    unit = lambda i: (i // (nb * dil), (i // dil) % nb, i % dil)

    def cur(i):
        bi, n, r = unit(i)
        return bi, r, n, 0

    def prev(i):
        bi, n, r = unit(i)
        return bi, r, jnp.maximum(n - 1, 0), 0

    blk = lambda f: pl.BlockSpec((None, None, SPAN, D_GROUP), f)
    in_specs = [blk(cur), blk(prev), blk(cur), blk(prev), blk(cur)]
    out_specs = [pl.BlockSpec((None, GROUP_SLABS, SPAN * dil, LANES), lambda i: (unit(i)[0], 0, unit(i)[1], 0)),
                 pl.BlockSpec((None, SPAN * dil, LANES), lambda i: (unit(i)[0], unit(i)[1], 0))]
    out_shape = [jax.ShapeDtypeStruct((b, GROUP_SLABS, t, LANES), F32),
                 jax.ShapeDtypeStruct((b, t, LANES), F32)]
    return b * nb * dil, nb, in_specs, out_specs, out_shape


def _merge_groups(o_list, lse_list):
    r = lax.broadcasted_iota(jnp.int32, (LANES, D_GROUP), 0)
    c = lax.broadcasted_iota(jnp.int32, (LANES, D_GROUP), 1)
    expand = jnp.where(c // HEAD_DIM == r, 1.0, 0.0).astype(BF16)
    top = jnp.maximum(jnp.maximum(lse_list[0], lse_list[1]), lse_list[2])
    e = [jnp.exp(l - top) for l in lse_list]
    den = e[0] + e[1] + e[2]
    out = None
    for eg, og in zip(e, o_list):
        w = eg / den
        hi = w.astype(BF16)
        lo = (w - hi.astype(F32)).astype(BF16)
        term = (_dot(hi, expand) + _dot(lo, expand)) * og
        out = term if out is None else out + term
    return out


def _gated_ffn(project, conv_taps, wconv_ref, bconv_ref, wdown_ref, rows, row_groups):
    n = rows // row_groups
    for r in range(row_groups):
        project(r * n, n)
    out = []
    for r in range(row_groups):
        halves = []
        for cs in (slice(0, D_FF), slice(D_FF, 2 * D_FF)):
            hc = bconv_ref[:, cs]
            for j in range(CONV_W):
                hc = hc + conv_taps(j, r * n, n, cs) * wconv_ref[j:j + 1, cs]
            halves.append(hc)
        gated = (_gelu_tanh(halves[0]) * halves[1]).astype(BF16)
        out.append(_dot(gated, wdown_ref[...]))
    return out[0] if row_groups == 1 else jnp.concatenate(out, axis=0)


def _back_kernel(x_ref, mod_ref, gp_ref, sg_ref, o0, o1, o2, l0, l1, l2,
                 wpa_ref, wout_ref, wup_ref, wconv_ref, bconv_ref, wdown_ref,
                 gpost_ref, gpre_ref, gffn_ref, y_ref, ctail_ref, hbuf):
    i = pl.program_id(1)
    tm = x_ref.shape[1]

    @pl.when((pl.program_id(0) == 0) & (i == 0))
    def _():
        hbuf[tm:tm + CONV_HALO, :] = jnp.zeros((CONV_HALO, 2 * D_FF), F32)

    mod = lambda r: mod_ref[0, r:r + 1, :]
    o_list = [jnp.concatenate([ref[j] for j in range(GROUP_SLABS)], axis=-1) for ref in (o0, o1, o2)]
    o = _merge_groups(o_list, [l0[0], l1[0], l2[0]])
    y_att = _dot(o.astype(BF16), wpa_ref[...])
    mix = gp_ref[0].astype(F32) + sg_ref[0].astype(F32) * y_att
    x1 = x_ref[0] + mod(2) * _rms(_dot(mix.astype(BF16), wout_ref[...]), gpost_ref[...])
    h2 = (_rms(x1, gpre_ref[...]) * (1.0 + mod(4)) + mod(3)).astype(BF16)

    hbuf[0:CONV_HALO, :] = jnp.where(i > 0, hbuf[tm:tm + CONV_HALO, :], 0.0)

    def project(r0, n):
        for cs in (slice(0, D_FF), slice(D_FF, 2 * D_FF)):
            hbuf[CONV_HALO + r0:CONV_HALO + r0 + n, cs] = _dot(h2[r0:r0 + n], wup_ref[:, cs])

    def taps(j, r0, n, cs):
        off = CONV_HALO - (CONV_W - 1) + j + r0
        return hbuf[off:off + n, cs]

    ffn = _gated_ffn(project, taps, wconv_ref, bconv_ref, wdown_ref, tm, tm // FFN_ROWS)
    y_ref[0] = x1 + mod(5) * _rms(ffn, gffn_ref[...])
    ctail_ref[0] = hbuf[tm:tm + CONV_HALO, :]


def _back(x, mod, gp, sg, o_list, lse_list, w_pa, w_out, w_up, w_conv, b_conv, w_down,
          g_post, g_pre, g_ffn):
    b, t, d = x.shape
    tm = TM_BACK
    row = lambda shape: pl.BlockSpec(shape, lambda bi, i: (bi, i, 0))
    slab = pl.BlockSpec((None, GROUP_SLABS, tm, LANES), lambda bi, i: (bi, 0, i, 0))
    in_specs = ([row((1, tm, d)), pl.BlockSpec((1, 6, d), lambda bi, i: (bi, 0, 0)),
                 row((1, tm, d)), row((1, tm, d))]
                + [slab] * N_GROUPS + [row((1, tm, LANES))] * N_GROUPS
                + [_const_spec(w.shape) for w in (w_pa, w_out, w_up, w_conv, b_conv, w_down)]
                + [_const_spec((1, d))] * 3)
    return pl.pallas_call(
        _back_kernel,
        grid=(b, t // tm),
        in_specs=in_specs,
        out_specs=[row((1, tm, d)), pl.BlockSpec((1, CONV_HALO, 2 * D_FF), lambda bi, i: (bi, 0, 0))],
        out_shape=[jax.ShapeDtypeStruct((b, t, d), F32),
                   jax.ShapeDtypeStruct((b, CONV_HALO, 2 * D_FF), F32)],
        scratch_shapes=[pltpu.VMEM((CONV_HALO + tm, 2 * D_FF), F32)],
        compiler_params=_params("arbitrary", "arbitrary"),
        name="back",
    )(x, mod, gp, sg, *o_list, *lse_list, w_pa, w_out, w_up, w_conv, b_conv, w_down,
      g_post, g_pre, g_ffn)


def _sfront_kernel(x_ref, shift_ref, scale_ref, g_ref, w_ref, o_ref, h_scr):
    @pl.when(pl.program_id(0) == 0)
    def _():
        h = _rms(x_ref[...], g_ref[...]) * (1.0 + scale_ref[...]) + shift_ref[...]
        h_scr[...] = h.astype(BF16)

    o_ref[...] = _dot(h_scr[...], w_ref[...])


def _sfront(x, mod, g_pre, w_in):
    m, d = x.shape
    n = w_in.shape[1]
    tn = 1536
    return pl.pallas_call(
        _sfront_kernel,
        grid=(n // tn,),
        in_specs=[pl.BlockSpec((m, d), lambda j: (0, 0)),
                  pl.BlockSpec((m, d), lambda j: (0, 0)),
                  pl.BlockSpec((m, d), lambda j: (0, 1)),
                  pl.BlockSpec((1, d), lambda j: (0, 0)),
                  pl.BlockSpec((d, tn), lambda j: (0, j))],
        out_specs=pl.BlockSpec((m, tn), lambda j: (0, j)),
        out_shape=jax.ShapeDtypeStruct((m, n), F32),
        scratch_shapes=[pltpu.VMEM((m, d), BF16)],
        compiler_params=_params("arbitrary"),
        name="sfront",
    )(x, mod, mod, g_pre, w_in)


def _sattn_kernel(*refs, dil, units):
    qrow_ref, q_ref, kn_ref, vn_ref, knt_ref, vnt_ref, kt_ref, vt_ref = refs[:N_SATTN_IN]
    n_in = N_SATTN_IN + 5 * len(units)
    o_ref, lse_ref, kto_ref, vto_ref = refs[n_in:n_in + 4]
    step = pl.program_id(0)
    side_tasks = []
    for g, (nb, unit_dil) in enumerate(units):
        ins = refs[N_SATTN_IN + 5 * g:N_SATTN_IN + 5 * g + 5]
        side_tasks += _attn_unit_tasks(*ins, *refs[n_in + 4 + 2 * g:n_in + 6 + 2 * g],
                                       (step // unit_dil) % nb, step % unit_dil, unit_dil)

    bb, n_slots, dh, length = kt_ref.shape
    lane = lax.broadcasted_iota(jnp.int32, (1, length), 1)
    attended = (lane & (dil - 1)) == 0
    last = lane == length - 1
    slot = lax.broadcasted_iota(jnp.int32, (n_slots, 1), 0)
    state = [dict(sc=jnp.zeros((n_slots, length), F32), o=jnp.zeros((n_slots, dh), F32)) for _ in range(bb)]

    def score_slot(b, s):
        kt = kt_ref[b, s]
        state[b]["sc"] = state[b]["sc"] + _dot(qrow_ref[b, s], kt.astype(BF16))
        kto_ref[b, s] = jnp.where(last, knt_ref[b, :, s:s + 1], pltpu.roll(kt, length - 1, 1))

    def softmax(b):
        sc_new = jnp.sum(kn_ref[b] * q_ref[b], axis=-1, keepdims=True) * Q_SCALE
        sc = jnp.where(attended, state[b]["sc"], NEG)
        m = jnp.maximum(jnp.max(sc, axis=-1, keepdims=True), sc_new)
        p = jnp.exp(sc - m)
        p_new = jnp.exp(sc_new - m)
        den = jnp.sum(p, axis=-1, keepdims=True) + p_new
        lse_ref[b] = m + jnp.log(den)
        state[b].update(p=p.astype(BF16), p_new=p_new, inv=1.0 / den)

    def value_slot(b, s):
        vt = vt_ref[b, s]
        pv = lax.dot_general(state[b]["p"], vt.astype(BF16), (((1,), (1,)), ((), ())),
                             preferred_element_type=F32)
        state[b]["o"] = jnp.where(slot == s, pv, state[b]["o"])
        vto_ref[b, s] = jnp.where(last, vnt_ref[b, :, s:s + 1], pltpu.roll(vt, length - 1, 1))

    def finish(b):
        o_ref[b] = (state[b]["o"] + vn_ref[b] * state[b]["p_new"]) * state[b]["inv"]

    for b in range(bb):
        for s in range(n_slots):
            score_slot(b, s)
        softmax(b)
        for s in range(n_slots):
            value_slot(b, s)
        finish(b)
    for task in side_tasks:
        task()


def _sattn(q, kn, vn, kt, vt, dil, prompt=()):
    b, n_slots, dh, length = kt.shape
    bb = max(1, CACHE_BLOCK_BYTES // (n_slots * dh * length * 4))
    steps = b // bb
    own_row = jnp.eye(n_slots, dtype=F32)[None, :, :, None]
    qrow = (own_row * (q * Q_SCALE)[:, :, None, :]).astype(BF16)
    new = pl.BlockSpec((bb, n_slots, dh), lambda i: (i, 0, 0))
    column = pl.BlockSpec((bb, dh, n_slots), lambda i: (i, 0, 0))
    cache = pl.BlockSpec((bb, n_slots, dh, length), lambda i: (i, 0, 0, 0))
    in_specs = [pl.BlockSpec((bb, n_slots, n_slots, dh), lambda i: (i, 0, 0, 0)),
                new, new, new, column, column, cache, cache]
    assert len(in_specs) == N_SATTN_IN
    out_specs = [new, pl.BlockSpec((bb, n_slots, 1), lambda i: (i, 0, 0)), cache, cache]
    out_shape = [jax.ShapeDtypeStruct((b, n_slots, dh), F32),
                 jax.ShapeDtypeStruct((b, n_slots, 1), F32),
                 jax.ShapeDtypeStruct(kt.shape, F32), jax.ShapeDtypeStruct(vt.shape, F32)]
    args = [qrow, q, kn, vn, kn.transpose(0, 2, 1), vn.transpose(0, 2, 1), kt, vt]
    units = []
    for pq, pk, pv, pdil in prompt:
        n_units, nb, ins, outs, shapes = _attn_unit_specs(pq, pdil)
        assert n_units == steps, (n_units, steps)
        units.append((nb, pdil))
        in_specs += ins
        out_specs += outs
        out_shape += shapes
        args += [pq, pk, pk, pv, pv]
    return pl.pallas_call(
        functools.partial(_sattn_kernel, dil=dil, units=tuple(units)),
        grid=(steps,),
        in_specs=in_specs, out_specs=out_specs, out_shape=out_shape,
        compiler_params=_params("arbitrary"),
        name="sattn",
    )(*args)


def _pool_state_kernel(hist_ref, u_ref, o_ref):
    newest = pl.program_id(0) == POOL_HIST - 1
    o_ref[0] = jnp.where(newest, u_ref[...], hist_ref[0])


def _pool_state(hist_t, proj):
    steps, m, d = hist_t.shape
    return pl.pallas_call(
        _pool_state_kernel,
        grid=(steps,),
        in_specs=[pl.BlockSpec((1, m, d), lambda j: (jnp.minimum(j + 1, steps - 1), 0, 0)),
                  pl.BlockSpec((m, d), lambda j: (0, COL_U // d))],
        out_specs=pl.BlockSpec((1, m, d), lambda j: (j, 0, 0)),
        out_shape=jax.ShapeDtypeStruct(hist_t.shape, F32),
        compiler_params=_params("arbitrary"),
        name="pool_state",
    )(hist_t, proj)


def _sback_kernel(x_ref, mod_ref, proj_ref, hist_ref, o0, o1, o2, l0, l1, l2, chist_ref,
                  wmap_ref, pscale_ref, wpp_ref, wpa_ref, wout_ref, wup_ref, wconv_ref, bconv_ref,
                  wdown_ref, gpost_ref, gpre_ref, gffn_ref, y_ref, hu_ref):
    m = x_ref.shape[0]
    mod = lambda r: mod_ref[:, r * D_MODEL:(r + 1) * D_MODEL]
    u = proj_ref[:, COL_U:COL_Q]
    mixed = []
    for g, win in enumerate(POOL_WINDOWS):
        lo, hi = g * POOL_GROUP, (g + 1) * POOL_GROUP
        wsum = u[:, lo:hi]
        for j in range(1, win):
            wsum = wsum + hist_ref[POOL_HIST - j, :, lo:hi]
        pooled = wsum / float(win) - u[:, lo:hi]
        mixed.append(_dot(pooled.astype(BF16), wmap_ref[g]))
    mixed = jnp.concatenate(mixed, axis=-1) * pscale_ref[...]
    y_pool = _dot(mixed.astype(BF16), wpp_ref[...])
    o = _merge_groups([o0[...], o1[...], o2[...]], [l0[...], l1[...], l2[...]])
    y_att = _dot(o.astype(BF16), wpa_ref[...])
    mix = (_sigmoid(proj_ref[:, COL_ZP:COL_ZA]) * y_pool + _sigmoid(proj_ref[:, COL_ZA:D_IN]) * y_att)
    x1 = x_ref[...] + mod(2) * _rms(_dot(mix.astype(BF16), wout_ref[...]), gpost_ref[...])
    h2 = (_rms(x1, gpre_ref[...]) * (1.0 + mod(4)) + mod(3)).astype(BF16)

    def project(r0, n):
        hu_ref[...] = _dot(h2, wup_ref[...])

    def taps(j, r0, n, cs):
        return hu_ref[:, cs] if j == CONV_W - 1 else chist_ref[j, :, cs]

    ffn = _gated_ffn(project, taps, wconv_ref, bconv_ref, wdown_ref, m, 1)
    y_ref[...] = x1 + mod(5) * _rms(ffn, gffn_ref[...])


def _sback(x, mod, proj, hist_t, o_list, lse_list, chist_t, weights, gains):
    m, d = x.shape
    args = [x, mod, proj, hist_t, *o_list, *lse_list, chist_t, *weights, *gains]
    full = lambda shape: pl.BlockSpec(shape, lambda i: (0,) * len(shape))
    out_shapes = [(m, d), (m, 2 * D_FF)]
    return pl.pallas_call(
        _sback_kernel,
        grid=(1,),
        in_specs=[_const_spec(a.shape) for a in args],
        out_specs=[full(s) for s in out_shapes],
        out_shape=[jax.ShapeDtypeStruct(s, F32) for s in out_shapes],
        compiler_params=_params("arbitrary"),
        name="sback",
    )(*args)


def kernel(x_prompt, x_sample, c_prompt, c_sample, cache_k_w128, cache_v_w128, cache_k_w512, cache_v_w512,
           cache_k_w2048, cache_v_w2048, state_pool, state_conv, w_ada, b_ada, g_pre_mix, g_post_mix,
           g_pre_ffn, g_post_ffn, w_in, w_pool_map, pool_scale, w_proj_pool, w_proj_att, w_out, w_up,
           w_conv, b_conv, w_down):
    bp, tp, d = x_prompt.shape
    bs = x_sample.shape[0]
    (w_ada, b_ada, g_pre_mix, g_post_mix, g_pre_ffn, g_post_ffn, w_in, w_pool_map, pool_scale,
     w_proj_pool, w_proj_att, w_out, w_up, w_conv, b_conv, w_down) = (
        w[0] for w in (w_ada, b_ada, g_pre_mix, g_post_mix, g_pre_ffn, g_post_ffn, w_in, w_pool_map,
                       pool_scale, w_proj_pool, w_proj_att, w_out, w_up, w_conv, b_conv, w_down))
    caches = [(cache_k_w128[0], cache_v_w128[0]), (cache_k_w512[0], cache_v_w512[0]),
              (cache_k_w2048[0], cache_v_w2048[0])]
    g_pre_mix, g_post_mix, g_pre_ffn, g_post_ffn, pool_scale, b_conv, b_ada = (
        a.reshape(1, -1) for a in (g_pre_mix, g_post_mix, g_pre_ffn, g_post_ffn, pool_scale, b_conv, b_ada))
    w_in_b, w_map_b, w_pp_b, w_pa_b, w_out_b, w_up_b, w_down_b = (
        w.astype(BF16) for w in (w_in, w_pool_map, w_proj_pool, w_proj_att, w_out, w_up, w_down))

    mod = _ada(jnp.concatenate([c_prompt, c_sample], axis=0), w_ada, b_ada)
    mod_p = mod[:bp].reshape(bp, 6, d)
    mod_s = mod[bp:]

    front = _front(x_prompt, mod_p, g_pre_mix, w_in_b, w_map_b, pool_scale, w_pp_b)
    gp, sg = front[0], front[1]
    qkv = front[2:11]
    kv_t = front[11:17]
    u_tail = front[17]
    prompt_units = [(qkv[3 * g], qkv[3 * g + 1], qkv[3 * g + 2], dil) for g, (_, dil) in enumerate(ATT_GROUPS)]

    proj_s = _sfront(x_sample[:, 0], mod_s, g_pre_mix, w_in_b)
    heads = lambda lo: proj_s[:, lo:lo + D_ATT].reshape(bs, N_GROUPS, N_SLOTS, HEAD_DIM).transpose(1, 0, 2, 3)
    q_s, k_s, v_s = heads(COL_Q), heads(COL_K), heads(COL_V)
    o_s, lse_s, new_caches = [], [], []
    for g, ((_, dil), (kc, vc)) in enumerate(zip(ATT_GROUPS, caches)):
        kt, vt = kc.transpose(0, 2, 3, 1), vc.transpose(0, 2, 3, 1)
        if g == N_GROUPS - 1:
            res = _sattn(q_s[g], k_s[g], v_s[g], kt, vt, dil, prompt_units)
            o_list, lse_list = list(res[4::2]), list(res[5::2])
        else:
            res = _sattn(q_s[g], k_s[g], v_s[g], kt, vt, dil)
        o_g, lse_g, kt_new, vt_new = res[:4]
        o_s.append(o_g.reshape(bs, D_GROUP))
        lse_s.append(jnp.pad(lse_g[:, :, 0], ((0, 0), (0, LANES - N_SLOTS))))
        new_caches += [kt_new.transpose(0, 3, 1, 2)[None], vt_new.transpose(0, 3, 1, 2)[None]]
    hist_t = state_pool[0].transpose(1, 0, 2)
    chist_t = state_conv[0].transpose(1, 0, 2)
    pool_state_t = _pool_state(hist_t, proj_s)
    y_s, hu_s = _sback(
        x_sample[:, 0], mod_s, proj_s, hist_t, o_s, lse_s, chist_t,
        (w_map_b, pool_scale, w_pp_b, w_pa_b, w_out_b, w_up_b, w_conv, b_conv, w_down_b),
        (g_post_mix, g_pre_ffn, g_post_ffn))
    conv_state_s = jnp.concatenate([state_conv[0][:, 1:], hu_s[:, None]], axis=1)

    y_p, conv_tail = _back(x_prompt, mod_p, gp, sg, o_list, lse_list, w_pa_b, w_out_b, w_up_b, w_conv,
                           b_conv, w_down_b, g_post_mix, g_pre_ffn, g_post_ffn)

    outs = [y_p, y_s[:, None]]
    for g in range(N_GROUPS):
        for j in range(2):
            f = kv_t[2 * g + j]
            outs.append(f.reshape(bp, N_SLOTS, HEAD_DIM, f.shape[2]).transpose(0, 3, 1, 2)[None])
            outs.append(new_caches[2 * g + j])
    outs += [u_tail[None, :, HALO - POOL_HIST:], pool_state_t.transpose(1, 0, 2)[None],
             conv_tail[None, :, CONV_HALO - (CONV_W - 1):], conv_state_s[None]]
    return tuple(outs)
```

```python
import functools

import jax
import jax.numpy as jnp
from jax import lax
from jax.experimental import pallas as pl
from jax.experimental.pallas import tpu as pltpu

F32 = jnp.float32
BF16 = jnp.bfloat16

D_MODEL = 1024
EPS = 1e-6
POOL_WINDOWS = (2, 4, 8, 16)
POOL_GROUP = D_MODEL // len(POOL_WINDOWS)
POOL_HIST = max(POOL_WINDOWS) - 1
ATT_GROUPS = ((128, 1), (512, 4), (2048, 16))
SPAN = 128
N_GROUPS = len(ATT_GROUPS)
N_SLOTS = 8
HEAD_DIM = 64
D_GROUP = N_SLOTS * HEAD_DIM
D_ATT = N_GROUPS * D_GROUP
D_FF = 2816
CONV_W = 3
COL_U, COL_Q, COL_K, COL_V, COL_ZP, COL_ZA, D_IN = 0, 1024, 2560, 4096, 5632, 6656, 7680
Q_SCALE = HEAD_DIM ** -0.5
NEG = -1e30

LANES = 128
GROUP_SLABS = D_GROUP // LANES
HALO = 16
CONV_HALO = 8
TM = 256
TM_BACK = 512
FFN_ROWS = 512
VMEM_LIMIT = 62 * 1024 * 1024
CACHE_BLOCK_BYTES = 4 * 1024 * 1024
N_SATTN_IN, N_SATTN_OUT = 8, 4


def _const_spec(shape):
    nd = len(shape)
    return pl.BlockSpec(shape, lambda *_: (0,) * nd, pipeline_mode=pl.Buffered(1))


def _rms(x, g):
    return x * lax.rsqrt(jnp.mean(x * x, axis=-1, keepdims=True) + EPS) * g


def _sigmoid(x):
    return 1.0 / (1.0 + jnp.exp(-x))


def _gelu_tanh(x):
    return 0.5 * x * (1.0 + jnp.tanh(0.7978845608028654 * (x + 0.044715 * (x * x * x))))


def _dot(a, b):
    return jnp.dot(a, b, preferred_element_type=F32)


def _params(*semantics):
    return pltpu.CompilerParams(dimension_semantics=semantics, vmem_limit_bytes=VMEM_LIMIT)


def _ada_kernel(c_ref, w_ref, b_ref, o_ref):
    c = c_ref[...]
    s = c * _sigmoid(c)
    o_ref[...] = _dot(s.astype(BF16), w_ref[...].astype(BF16)) + b_ref[...]


def _ada(c_all, w_ada, b_ada):
    m = c_all.shape[0]
    n = w_ada.shape[1]
    tn = 1536
    return pl.pallas_call(
        _ada_kernel,
        grid=(n // tn,),
        in_specs=[pl.BlockSpec((m, D_MODEL), lambda j: (0, 0)),
                  pl.BlockSpec((D_MODEL, tn), lambda j: (0, j)),
                  pl.BlockSpec((1, tn), lambda j: (0, j))],
        out_specs=pl.BlockSpec((m, tn), lambda j: (0, j)),
        out_shape=jax.ShapeDtypeStruct((m, n), F32),
        compiler_params=_params("arbitrary"),
        name="ada",
    )(c_all, w_ada, b_ada)


N_FRONT_IN, N_FRONT_OUT = 7, 18


def _front_kernel(*refs, cache_dils):
    n_in = N_FRONT_IN + N_SATTN_IN * len(cache_dils)
    x_ref, mod_ref, g_ref, win_ref, wmap_ref, pscale_ref, wpp_ref = refs[:N_FRONT_IN]
    (gp_ref, sg_ref, q0, k0, v0, q1, k1, v1, q2, k2, v2,
     kf0, vf0, kf1, vf1, kf2, vf2, utail_ref) = refs[n_in:n_in + N_FRONT_OUT]
    ubuf, slabs = refs[-2:]
    i = pl.program_id(1)
    tm = x_ref.shape[1]

    @pl.when((pl.program_id(0) == 0) & (i == 0))
    def _():
        ubuf[tm:tm + HALO, :] = jnp.zeros((HALO, D_MODEL), F32)

    hosted = []
    for c, cache_dil in enumerate(cache_dils):
        c_in = refs[N_FRONT_IN + c * N_SATTN_IN:N_FRONT_IN + (c + 1) * N_SATTN_IN]
        c_out = refs[n_in + N_FRONT_OUT + c * N_SATTN_OUT:n_in + N_FRONT_OUT + (c + 1) * N_SATTN_OUT]
        hosted.append((range(c_in[-1].shape[0]), _cache_update(c_in, c_out, cache_dil)))
    for rows, (c_scores, _, _) in hosted:
        for b in rows:
            c_scores(b)

    x = x_ref[0]
    shift = mod_ref[0, 0:1, :]
    scale = mod_ref[0, 1:2, :]
    h = (_rms(x, g_ref[...]) * (1.0 + scale) + shift).astype(BF16)

    u = _dot(h, win_ref[:, COL_U:COL_Q])
    ubuf[0:HALO, :] = jnp.where(i > 0, ubuf[tm:tm + HALO, :], 0.0)
    ubuf[HALO:HALO + tm, :] = u
    utail_ref[0] = u[tm - HALO:, :]

    pos = i * tm + lax.broadcasted_iota(jnp.int32, (tm, 1), 0)
    pooled = []
    for g, win in enumerate(POOL_WINDOWS):
        lo, hi = g * POOL_GROUP, (g + 1) * POOL_GROUP
        wsum = ubuf[HALO:HALO + tm, lo:hi]
        for j in range(1, win):
            wsum = wsum + ubuf[HALO - j:HALO - j + tm, lo:hi]
        count = jnp.minimum(pos + 1, win).astype(F32)
        pooled.append((wsum / count - ubuf[HALO:HALO + tm, lo:hi]).astype(BF16))

    sg_ref[0] = _sigmoid(_dot(h, win_ref[:, COL_ZA:D_IN])).astype(BF16)
    pool_gate = _sigmoid(_dot(h, win_ref[:, COL_ZP:COL_ZA]))

    def emit(val, refs, frefs):
        for j in range(D_ATT // LANES):
            slabs[j] = val[:, j * LANES:(j + 1) * LANES]
        for g, (_, dil) in enumerate(ATT_GROUPS):
            if dil == 1:
                refs[g][0, 0] = val[:, g * D_GROUP:(g + 1) * D_GROUP].astype(BF16)
                continue
            for r in range(dil):
                for jj in range(GROUP_SLABS):
                    piece = slabs[g * GROUP_SLABS + jj, pl.ds(r, tm // dil, stride=dil), :]
                    refs[g][0, r, :, jj * LANES:(jj + 1) * LANES] = piece.astype(BF16)
        for g, fref in enumerate(frefs):
            rows = fref.shape[2]
            fref[0] = val[tm - rows:, g * D_GROUP:(g + 1) * D_GROUP].T

    emit(_dot(h, win_ref[:, COL_Q:COL_K]) * Q_SCALE, (q0, q1, q2), ())
    for rows, (_, c_softmax, c_values) in hosted:
        for b in rows:
            c_softmax(b)
        for b in rows:
            c_values(b)
    emit(_dot(h, win_ref[:, COL_K:COL_V]), (k0, k1, k2), (kf0, kf1, kf2))
    emit(_dot(h, win_ref[:, COL_V:COL_ZP]), (v0, v1, v2), (vf0, vf1, vf2))

    mixed = [_dot(p, wmap_ref[g]) for g, p in enumerate(pooled)]
    mixed = jnp.concatenate(mixed, axis=-1) * pscale_ref[...]
    y_pool = _dot(mixed.astype(BF16), wpp_ref[...])
    gp_ref[0] = (pool_gate * y_pool).astype(BF16)


def _hosted_caches(caches, steps_per_batch, n_steps):
    args, ins, outs, shapes, dils = [], [], [], [], []
    for q, kn, vn, kt, vt, dil in caches:
        bb = kt.shape[0] // n_steps
        assert bb * n_steps == kt.shape[0]
        ops = _cache_operands(q, kn, vn, kt, vt, bb, lambda bi, i: bi * steps_per_batch + i)
        for acc, new in zip((args, ins, outs, shapes), ops):
            acc += new
        dils.append(dil)
    return args, ins, outs, shapes, tuple(dils)


def _front(x, mod, g_pre, w_in, w_map, pool_scale, w_pp, caches=()):
    b, t, d = x.shape
    tm = TM
    nblk = t // tm
    c_args, c_ins, c_outs, c_shapes, cache_dils = _hosted_caches(caches, nblk, b * nblk)
    row = lambda shape: pl.BlockSpec(shape, lambda bi, i: (bi, i, 0))
    in_specs = [row((1, tm, d)),
                pl.BlockSpec((1, 6, d), lambda bi, i: (bi, 0, 0)),
                _const_spec((1, d)), _const_spec(w_in.shape), _const_spec(w_map.shape),
                _const_spec((1, d)), _const_spec(w_pp.shape)]
    out_shape = [jax.ShapeDtypeStruct((b, t, d), BF16), jax.ShapeDtypeStruct((b, t, d), BF16)]
    out_specs = [row((1, tm, d)), row((1, tm, d))]
    for _, dil in ATT_GROUPS:
        for _ in range(3):
            out_shape.append(jax.ShapeDtypeStruct((b, dil, t // dil, D_GROUP), BF16))
            out_specs.append(pl.BlockSpec((1, dil, tm // dil, D_GROUP), lambda bi, i: (bi, 0, i, 0)))
    for win, _ in ATT_GROUPS:
        keep = min(win, t)
        rows = min(keep, tm)
        first = nblk - keep // rows
        for _ in range(2):
            out_shape.append(jax.ShapeDtypeStruct((b, D_GROUP, keep), F32))
            out_specs.append(pl.BlockSpec(
                (1, D_GROUP, rows), lambda bi, i, first=first: (bi, 0, jnp.maximum(i - first, 0))))
    out_shape.append(jax.ShapeDtypeStruct((b, HALO, d), F32))
    out_specs.append(pl.BlockSpec((1, HALO, d), lambda bi, i: (bi, 0, 0)))
    assert (len(in_specs), len(out_specs)) == (N_FRONT_IN, N_FRONT_OUT)
    return pl.pallas_call(
        functools.partial(_front_kernel, cache_dils=cache_dils),
        grid=(b, nblk),
        in_specs=in_specs + c_ins, out_specs=out_specs + c_outs, out_shape=out_shape + c_shapes,
        scratch_shapes=[pltpu.VMEM((HALO + tm, d), F32),
                        pltpu.VMEM((D_ATT // LANES, tm, LANES), F32)],
        compiler_params=_params("arbitrary", "arbitrary"),
        name="front",
    )(x, mod, g_pre, w_in, w_map, pool_scale, w_pp, *c_args)


def _attn_unit_tasks(q_ref, kp_ref, kc_ref, vp_ref, vc_ref, o_ref, lse_ref, n, r, dil):
    s2 = 2 * SPAN
    lane = lax.broadcasted_iota(jnp.int32, (SPAN, LANES), 1)
    low = lane < HEAD_DIM
    rows = pl.ds(r, SPAN, stride=dil) if dil > 1 else slice(None)
    state = {"lse": jnp.zeros((SPAN, LANES), F32)}

    def pair(j):
        row = lax.broadcasted_iota(jnp.int32, (s2, s2), 0) % SPAN
        col = lax.broadcasted_iota(jnp.int32, (s2, s2), 1)
        first_row = row + jnp.where(n > 0, 0, s2)
        mask = jnp.where(col < SPAN, col - first_row, row - (col - SPAN)) >= 0
        sl = slice(j * LANES, (j + 1) * LANES)
        q2 = q_ref[:, sl]
        zero = jnp.zeros_like(q2)
        qs = jnp.concatenate([jnp.where(low, q2, zero), jnp.where(low, zero, q2)], axis=0)
        kk = jnp.concatenate([kp_ref[:, sl], kc_ref[:, sl]], axis=0)
        vv = jnp.concatenate([vp_ref[:, sl], vc_ref[:, sl]], axis=0)
        s = lax.dot_general(qs, kk, (((1,), (1,)), ((), ())), preferred_element_type=F32)
        s = jnp.where(mask, s, NEG)
        m = jnp.max(s, axis=-1, keepdims=True)
        p = jnp.exp(s - m)
        den = jnp.sum(p, axis=-1, keepdims=True)
        o2 = _dot(p.astype(BF16), vv) / den
        lse2 = m + jnp.log(den)
        o_ref[j, rows, :] = jnp.where(low, o2[:SPAN], o2[SPAN:])
        tile = jnp.where(lane == 2 * j, lse2[:SPAN], state["lse"])
        state["lse"] = jnp.where(lane == 2 * j + 1, lse2[SPAN:], tile)

    def finish():
        lse_ref[rows, :] = state["lse"]

    return [functools.partial(pair, j) for j in range(N_SLOTS // 2)] + [finish]


def _attn_unit_specs(q, dil):
    b, _, n_sub, _ = q.shape
    nb = n_sub // SPAN
    t = n_sub * dil
    unit = lambda i: (i // (nb * dil), (i // dil) % nb, i % dil)

    def cur(i):
        bi, n, r = unit(i)
        return bi, r, n, 0

    def prev(i):
        bi, n, r = unit(i)
        return bi, r, jnp.maximum(n - 1, 0), 0

    blk = lambda f: pl.BlockSpec((None, None, SPAN, D_GROUP), f)
    in_specs = [blk(cur), blk(prev), blk(cur), blk(prev), blk(cur)]
    out_specs = [pl.BlockSpec((None, GROUP_SLABS, SPAN * dil, LANES), lambda i: (unit(i)[0], 0, unit(i)[1], 0)),
                 pl.BlockSpec((None, SPAN * dil, LANES), lambda i: (unit(i)[0], unit(i)[1], 0))]
    out_shape = [jax.ShapeDtypeStruct((b, GROUP_SLABS, t, LANES), F32),
                 jax.ShapeDtypeStruct((b, t, LANES), F32)]
    return b * nb * dil, nb, in_specs, out_specs, out_shape


def _merge_groups(o_list, lse_list):
    r = lax.broadcasted_iota(jnp.int32, (LANES, D_GROUP), 0)
    c = lax.broadcasted_iota(jnp.int32, (LANES, D_GROUP), 1)
    expand = jnp.where(c // HEAD_DIM == r, 1.0, 0.0).astype(BF16)
    top = jnp.maximum(jnp.maximum(lse_list[0], lse_list[1]), lse_list[2])
    e = [jnp.exp(l - top) for l in lse_list]
    den = e[0] + e[1] + e[2]
    out = None
    for eg, og in zip(e, o_list):
        w = eg / den
        hi = w.astype(BF16)
        lo = (w - hi.astype(F32)).astype(BF16)
        term = (_dot(hi, expand) + _dot(lo, expand)) * og
        out = term if out is None else out + term
    return out


def _gated_ffn(project, conv_taps, wconv_ref, bconv_ref, wdown_ref, rows, row_groups):
    n = rows // row_groups
    for r in range(row_groups):
        project(r * n, n)
    out = []
    for r in range(row_groups):
        halves = []
        for cs in (slice(0, D_FF), slice(D_FF, 2 * D_FF)):
            hc = bconv_ref[:, cs]
            for j in range(CONV_W):
                hc = hc + conv_taps(j, r * n, n, cs) * wconv_ref[j:j + 1, cs]
            halves.append(hc)
        gated = (_gelu_tanh(halves[0]) * halves[1]).astype(BF16)
        out.append(_dot(gated, wdown_ref[...]))
    return out[0] if row_groups == 1 else jnp.concatenate(out, axis=0)


def _back_kernel(x_ref, mod_ref, gp_ref, sg_ref, o0, o1, o2, l0, l1, l2,
                 wpa_ref, wout_ref, wup_ref, wconv_ref, bconv_ref, wdown_ref,
                 gpost_ref, gpre_ref, gffn_ref, y_ref, ctail_ref, hbuf):
    i = pl.program_id(1)
    tm = x_ref.shape[1]

    @pl.when((pl.program_id(0) == 0) & (i == 0))
    def _():
        hbuf[tm:tm + CONV_HALO, :] = jnp.zeros((CONV_HALO, 2 * D_FF), F32)

    mod = lambda r: mod_ref[0, r:r + 1, :]
    o_list = [jnp.concatenate([ref[j] for j in range(GROUP_SLABS)], axis=-1) for ref in (o0, o1, o2)]
    o = _merge_groups(o_list, [l0[0], l1[0], l2[0]])
    y_att = _dot(o.astype(BF16), wpa_ref[...])
    mix = gp_ref[0].astype(F32) + sg_ref[0].astype(F32) * y_att
    x1 = x_ref[0] + mod(2) * _rms(_dot(mix.astype(BF16), wout_ref[...]), gpost_ref[...])
    h2 = (_rms(x1, gpre_ref[...]) * (1.0 + mod(4)) + mod(3)).astype(BF16)

    hbuf[0:CONV_HALO, :] = jnp.where(i > 0, hbuf[tm:tm + CONV_HALO, :], 0.0)

    def project(r0, n):
        for cs in (slice(0, D_FF), slice(D_FF, 2 * D_FF)):
            hbuf[CONV_HALO + r0:CONV_HALO + r0 + n, cs] = _dot(h2[r0:r0 + n], wup_ref[:, cs])

    def taps(j, r0, n, cs):
        off = CONV_HALO - (CONV_W - 1) + j + r0
        return hbuf[off:off + n, cs]

    ffn = _gated_ffn(project, taps, wconv_ref, bconv_ref, wdown_ref, tm, tm // FFN_ROWS)
    y_ref[0] = x1 + mod(5) * _rms(ffn, gffn_ref[...])
    ctail_ref[0] = hbuf[tm:tm + CONV_HALO, :]


def _back(x, mod, gp, sg, o_list, lse_list, w_pa, w_out, w_up, w_conv, b_conv, w_down,
          g_post, g_pre, g_ffn):
    b, t, d = x.shape
    tm = TM_BACK
    row = lambda shape: pl.BlockSpec(shape, lambda bi, i: (bi, i, 0))
    slab = pl.BlockSpec((None, GROUP_SLABS, tm, LANES), lambda bi, i: (bi, 0, i, 0))
    in_specs = ([row((1, tm, d)), pl.BlockSpec((1, 6, d), lambda bi, i: (bi, 0, 0)),
                 row((1, tm, d)), row((1, tm, d))]
                + [slab] * N_GROUPS + [row((1, tm, LANES))] * N_GROUPS
                + [_const_spec(w.shape) for w in (w_pa, w_out, w_up, w_conv, b_conv, w_down)]
                + [_const_spec((1, d))] * 3)
    return pl.pallas_call(
        _back_kernel,
        grid=(b, t // tm),
        in_specs=in_specs,
        out_specs=[row((1, tm, d)), pl.BlockSpec((1, CONV_HALO, 2 * D_FF), lambda bi, i: (bi, 0, 0))],
        out_shape=[jax.ShapeDtypeStruct((b, t, d), F32),
                   jax.ShapeDtypeStruct((b, CONV_HALO, 2 * D_FF), F32)],
        scratch_shapes=[pltpu.VMEM((CONV_HALO + tm, 2 * D_FF), F32)],
        compiler_params=_params("arbitrary", "arbitrary"),
        name="back",
    )(x, mod, gp, sg, *o_list, *lse_list, w_pa, w_out, w_up, w_conv, b_conv, w_down,
      g_post, g_pre, g_ffn)


def _sfront_kernel(x_ref, shift_ref, scale_ref, g_ref, w_ref, o_ref, h_scr):
    @pl.when(pl.program_id(0) == 0)
    def _():
        h = _rms(x_ref[...], g_ref[...]) * (1.0 + scale_ref[...]) + shift_ref[...]
        h_scr[...] = h.astype(BF16)

    o_ref[...] = _dot(h_scr[...], w_ref[...])


def _sfront(x, mod, g_pre, w_in):
    m, d = x.shape
    n = w_in.shape[1]
    tn = 1536
    return pl.pallas_call(
        _sfront_kernel,
        grid=(n // tn,),
        in_specs=[pl.BlockSpec((m, d), lambda j: (0, 0)),
                  pl.BlockSpec((m, d), lambda j: (0, 0)),
                  pl.BlockSpec((m, d), lambda j: (0, 1)),
                  pl.BlockSpec((1, d), lambda j: (0, 0)),
                  pl.BlockSpec((d, tn), lambda j: (0, j))],
        out_specs=pl.BlockSpec((m, tn), lambda j: (0, j)),
        out_shape=jax.ShapeDtypeStruct((m, n), F32),
        scratch_shapes=[pltpu.VMEM((m, d), BF16)],
        compiler_params=_params("arbitrary"),
        name="sfront",
    )(x, mod, mod, g_pre, w_in)


def _cache_update(in_refs, out_refs, dil):
    qrow_ref, q_ref, kn_ref, vn_ref, knt_ref, vnt_ref, kt_ref, vt_ref = in_refs
    o_ref, lse_ref, kto_ref, vto_ref = out_refs
    bb, n_slots, dh, length = kt_ref.shape
    lane = lax.broadcasted_iota(jnp.int32, (1, length), 1)
    attended = (lane & (dil - 1)) == 0
    last = lane == length - 1
    slot = lax.broadcasted_iota(jnp.int32, (n_slots, 1), 0)
    state = [dict(sc=jnp.zeros((n_slots, length), F32), o=jnp.zeros((n_slots, dh), F32)) for _ in range(bb)]

    def score_slot(b, s):
        kt = kt_ref[b, s]
        state[b]["sc"] = state[b]["sc"] + _dot(qrow_ref[b, s], kt.astype(BF16))
        kto_ref[b, s] = jnp.where(last, knt_ref[b, :, s:s + 1], pltpu.roll(kt, length - 1, 1))

    def softmax(b):
        sc_new = jnp.sum(kn_ref[b] * q_ref[b], axis=-1, keepdims=True) * Q_SCALE
        sc = jnp.where(attended, state[b]["sc"], NEG)
        m = jnp.maximum(jnp.max(sc, axis=-1, keepdims=True), sc_new)
        p = jnp.exp(sc - m)
        p_new = jnp.exp(sc_new - m)
        den = jnp.sum(p, axis=-1, keepdims=True) + p_new
        lse_ref[b] = m + jnp.log(den)
        state[b].update(p=p.astype(BF16), p_new=p_new, inv=1.0 / den)

    def value_slot(b, s):
        vt = vt_ref[b, s]
        pv = lax.dot_general(state[b]["p"], vt.astype(BF16), (((1,), (1,)), ((), ())),
                             preferred_element_type=F32)
        state[b]["o"] = jnp.where(slot == s, pv, state[b]["o"])
        vto_ref[b, s] = jnp.where(last, vnt_ref[b, :, s:s + 1], pltpu.roll(vt, length - 1, 1))

    def scores(b):
        for s in range(n_slots):
            score_slot(b, s)

    def values(b):
        for s in range(n_slots):
            value_slot(b, s)
        o_ref[b] = (state[b]["o"] + vn_ref[b] * state[b]["p_new"]) * state[b]["inv"]

    return scores, softmax, values


def _cache_operands(q, kn, vn, kt, vt, bb, block):
    b, n_slots, dh, length = kt.shape
    own_row = jnp.eye(n_slots, dtype=F32)[None, :, :, None]
    qrow = (own_row * (q * Q_SCALE)[:, :, None, :]).astype(BF16)
    spec = lambda *shape: pl.BlockSpec((bb,) + shape, lambda *ids: (block(*ids),) + (0,) * len(shape))
    new, column, cache = spec(n_slots, dh), spec(dh, n_slots), spec(n_slots, dh, length)
    in_specs = [spec(n_slots, n_slots, dh), new, new, new, column, column, cache, cache]
    assert len(in_specs) == N_SATTN_IN
    out_specs = [new, spec(n_slots, 1), cache, cache]
    out_shape = [jax.ShapeDtypeStruct((b, n_slots, dh), F32),
                 jax.ShapeDtypeStruct((b, n_slots, 1), F32),
                 jax.ShapeDtypeStruct(kt.shape, F32), jax.ShapeDtypeStruct(vt.shape, F32)]
    args = [qrow, q, kn, vn, kn.transpose(0, 2, 1), vn.transpose(0, 2, 1), kt, vt]
    return args, in_specs, out_specs, out_shape


def _sattn_kernel(*refs, dil, units):
    n_in = N_SATTN_IN + 5 * len(units)
    scores, softmax, values = _cache_update(refs[:N_SATTN_IN], refs[n_in:n_in + N_SATTN_OUT], dil)
    for b in range(refs[N_SATTN_IN - 1].shape[0]):
        scores(b)
        softmax(b)
        values(b)
    step = pl.program_id(0)
    for g, (nb, unit_dil) in enumerate(units):
        ins = refs[N_SATTN_IN + 5 * g:N_SATTN_IN + 5 * g + 5]
        outs = refs[n_in + N_SATTN_OUT + 2 * g:n_in + N_SATTN_OUT + 2 * g + 2]
        for task in _attn_unit_tasks(*ins, *outs, (step // unit_dil) % nb, step % unit_dil, unit_dil):
            task()


def _sattn(q, kn, vn, kt, vt, dil, prompt=()):
    b, n_slots, dh, length = kt.shape
    bb = max(1, CACHE_BLOCK_BYTES // (n_slots * dh * length * 4))
    steps = b // bb
    args, in_specs, out_specs, out_shape = _cache_operands(q, kn, vn, kt, vt, bb, lambda i: i)
    units = []
    for pq, pk, pv, pdil in prompt:
        n_units, nb, ins, outs, shapes = _attn_unit_specs(pq, pdil)
        assert n_units == steps, (n_units, steps)
        units.append((nb, pdil))
        in_specs += ins
        out_specs += outs
        out_shape += shapes
        args += [pq, pk, pk, pv, pv]
    return pl.pallas_call(
        functools.partial(_sattn_kernel, dil=dil, units=tuple(units)),
        grid=(steps,),
        in_specs=in_specs, out_specs=out_specs, out_shape=out_shape,
        compiler_params=_params("arbitrary"),
        name="sattn",
    )(*args)


def _pool_state_kernel(hist_ref, u_ref, o_ref):
    newest = pl.program_id(0) == POOL_HIST - 1
    o_ref[0] = jnp.where(newest, u_ref[...], hist_ref[0])


def _pool_state(hist_t, proj):
    steps, m, d = hist_t.shape
    return pl.pallas_call(
        _pool_state_kernel,
        grid=(steps,),
        in_specs=[pl.BlockSpec((1, m, d), lambda j: (jnp.minimum(j + 1, steps - 1), 0, 0)),
                  pl.BlockSpec((m, d), lambda j: (0, COL_U // d))],
        out_specs=pl.BlockSpec((1, m, d), lambda j: (j, 0, 0)),
        out_shape=jax.ShapeDtypeStruct(hist_t.shape, F32),
        compiler_params=_params("arbitrary"),
        name="pool_state",
    )(hist_t, proj)


def _sback_kernel(x_ref, mod_ref, proj_ref, hist_ref, o0, o1, o2, l0, l1, l2, chist_ref,
                  wmap_ref, pscale_ref, wpp_ref, wpa_ref, wout_ref, wup_ref, wconv_ref, bconv_ref,
                  wdown_ref, gpost_ref, gpre_ref, gffn_ref, y_ref, hu_ref):
    m = x_ref.shape[0]
    mod = lambda r: mod_ref[:, r * D_MODEL:(r + 1) * D_MODEL]
    u = proj_ref[:, COL_U:COL_Q]
    mixed = []
    for g, win in enumerate(POOL_WINDOWS):
        lo, hi = g * POOL_GROUP, (g + 1) * POOL_GROUP
        wsum = u[:, lo:hi]
        for j in range(1, win):
            wsum = wsum + hist_ref[POOL_HIST - j, :, lo:hi]
        pooled = wsum / float(win) - u[:, lo:hi]
        mixed.append(_dot(pooled.astype(BF16), wmap_ref[g]))
    mixed = jnp.concatenate(mixed, axis=-1) * pscale_ref[...]
    y_pool = _dot(mixed.astype(BF16), wpp_ref[...])
    o = _merge_groups([o0[...], o1[...], o2[...]], [l0[...], l1[...], l2[...]])
    y_att = _dot(o.astype(BF16), wpa_ref[...])
    mix = (_sigmoid(proj_ref[:, COL_ZP:COL_ZA]) * y_pool + _sigmoid(proj_ref[:, COL_ZA:D_IN]) * y_att)
    x1 = x_ref[...] + mod(2) * _rms(_dot(mix.astype(BF16), wout_ref[...]), gpost_ref[...])
    h2 = (_rms(x1, gpre_ref[...]) * (1.0 + mod(4)) + mod(3)).astype(BF16)

    def project(r0, n):
        hu_ref[...] = _dot(h2, wup_ref[...])

    def taps(j, r0, n, cs):
        return hu_ref[:, cs] if j == CONV_W - 1 else chist_ref[j, :, cs]

    ffn = _gated_ffn(project, taps, wconv_ref, bconv_ref, wdown_ref, m, 1)
    y_ref[...] = x1 + mod(5) * _rms(ffn, gffn_ref[...])


def _sback(x, mod, proj, hist_t, o_list, lse_list, chist_t, weights, gains):
    m, d = x.shape
    args = [x, mod, proj, hist_t, *o_list, *lse_list, chist_t, *weights, *gains]
    full = lambda shape: pl.BlockSpec(shape, lambda i: (0,) * len(shape))
    out_shapes = [(m, d), (m, 2 * D_FF)]
    return pl.pallas_call(
        _sback_kernel,
        grid=(1,),
        in_specs=[_const_spec(a.shape) for a in args],
        out_specs=[full(s) for s in out_shapes],
        out_shape=[jax.ShapeDtypeStruct(s, F32) for s in out_shapes],
        compiler_params=_params("arbitrary"),
        name="sback",
    )(*args)


def kernel(x_prompt, x_sample, c_prompt, c_sample, cache_k_w128, cache_v_w128, cache_k_w512, cache_v_w512,
           cache_k_w2048, cache_v_w2048, state_pool, state_conv, w_ada, b_ada, g_pre_mix, g_post_mix,
           g_pre_ffn, g_post_ffn, w_in, w_pool_map, pool_scale, w_proj_pool, w_proj_att, w_out, w_up,
           w_conv, b_conv, w_down):
    bp, tp, d = x_prompt.shape
    bs = x_sample.shape[0]
    (w_ada, b_ada, g_pre_mix, g_post_mix, g_pre_ffn, g_post_ffn, w_in, w_pool_map, pool_scale,
     w_proj_pool, w_proj_att, w_out, w_up, w_conv, b_conv, w_down) = (
        w[0] for w in (w_ada, b_ada, g_pre_mix, g_post_mix, g_pre_ffn, g_post_ffn, w_in, w_pool_map,
                       pool_scale, w_proj_pool, w_proj_att, w_out, w_up, w_conv, b_conv, w_down))
    caches = [(cache_k_w128[0], cache_v_w128[0]), (cache_k_w512[0], cache_v_w512[0]),
              (cache_k_w2048[0], cache_v_w2048[0])]
    g_pre_mix, g_post_mix, g_pre_ffn, g_post_ffn, pool_scale, b_conv, b_ada = (
        a.reshape(1, -1) for a in (g_pre_mix, g_post_mix, g_pre_ffn, g_post_ffn, pool_scale, b_conv, b_ada))
    w_in_b, w_map_b, w_pp_b, w_pa_b, w_out_b, w_up_b, w_down_b = (
        w.astype(BF16) for w in (w_in, w_pool_map, w_proj_pool, w_proj_att, w_out, w_up, w_down))

    mod = _ada(jnp.concatenate([c_prompt, c_sample], axis=0), w_ada, b_ada)
    mod_p = mod[:bp].reshape(bp, 6, d)
    mod_s = mod[bp:]

    proj_s = _sfront(x_sample[:, 0], mod_s, g_pre_mix, w_in_b)
    heads = lambda lo: proj_s[:, lo:lo + D_ATT].reshape(bs, N_GROUPS, N_SLOTS, HEAD_DIM).transpose(1, 0, 2, 3)
    q_s, k_s, v_s = heads(COL_Q), heads(COL_K), heads(COL_V)
    cache_ops = [(q_s[g], k_s[g], v_s[g], kc.transpose(0, 2, 3, 1), vc.transpose(0, 2, 3, 1), dil)
                 for g, ((_, dil), (kc, vc)) in enumerate(zip(ATT_GROUPS, caches))]

    front = _front(x_prompt, mod_p, g_pre_mix, w_in_b, w_map_b, pool_scale, w_pp_b, cache_ops[:2])
    gp, sg = front[0], front[1]
    qkv = front[2:11]
    kv_t = front[11:17]
    u_tail = front[17]
    prompt_units = [(qkv[3 * g], qkv[3 * g + 1], qkv[3 * g + 2], dil) for g, (_, dil) in enumerate(ATT_GROUPS)]

    res = _sattn(*cache_ops[2], prompt_units)
    o_list, lse_list = list(res[N_SATTN_OUT::2]), list(res[N_SATTN_OUT + 1::2])
    cache_res = [front[N_FRONT_OUT:N_FRONT_OUT + N_SATTN_OUT], front[N_FRONT_OUT + N_SATTN_OUT:],
                 res[:N_SATTN_OUT]]
    o_s, lse_s, new_caches = [], [], []
    for o_g, lse_g, kt_new, vt_new in cache_res:
        o_s.append(o_g.reshape(bs, D_GROUP))
        lse_s.append(jnp.pad(lse_g[:, :, 0], ((0, 0), (0, LANES - N_SLOTS))))
        new_caches += [kt_new.transpose(0, 3, 1, 2)[None], vt_new.transpose(0, 3, 1, 2)[None]]
    hist_t = state_pool[0].transpose(1, 0, 2)
    chist_t = state_conv[0].transpose(1, 0, 2)
    pool_state_t = _pool_state(hist_t, proj_s)
    y_s, hu_s = _sback(
        x_sample[:, 0], mod_s, proj_s, hist_t, o_s, lse_s, chist_t,
        (w_map_b, pool_scale, w_pp_b, w_pa_b, w_out_b, w_up_b, w_conv, b_conv, w_down_b),
        (g_post_mix, g_pre_ffn, g_post_ffn))
    conv_state_s = jnp.concatenate([state_conv[0][:, 1:], hu_s[:, None]], axis=1)

    y_p, conv_tail = _back(x_prompt, mod_p, gp, sg, o_list, lse_list, w_pa_b, w_out_b, w_up_b, w_conv,
                           b_conv, w_down_b, g_post_mix, g_pre_ffn, g_post_ffn)

    outs = [y_p, y_s[:, None]]
    for g in range(N_GROUPS):
        for j in range(2):
            f = kv_t[2 * g + j]
            outs.append(f.reshape(bp, N_SLOTS, HEAD_DIM, f.shape[2]).transpose(0, 3, 1, 2)[None])
            outs.append(new_caches[2 * g + j])
    outs += [u_tail[None, :, HALO - POOL_HIST:], pool_state_t.transpose(1, 0, 2)[None],
             conv_tail[None, :, CONV_HALO - (CONV_W - 1):], conv_state_s[None]]
    return tuple(outs)
```

```python
import functools

import jax
import jax.numpy as jnp
from jax import lax
from jax.experimental import pallas as pl
from jax.experimental.pallas import tpu as pltpu

F32 = jnp.float32
BF16 = jnp.bfloat16

D_MODEL = 1024
EPS = 1e-6
POOL_WINDOWS = (2, 4, 8, 16)
POOL_GROUP = D_MODEL // len(POOL_WINDOWS)
POOL_HIST = max(POOL_WINDOWS) - 1
ATT_GROUPS = ((128, 1), (512, 4), (2048, 16))
SPAN = 128
N_GROUPS = len(ATT_GROUPS)
N_SLOTS = 8
HEAD_DIM = 64
D_GROUP = N_SLOTS * HEAD_DIM
D_ATT = N_GROUPS * D_GROUP
D_FF = 2816
CONV_W = 3
COL_U, COL_Q, COL_K, COL_V, COL_ZP, COL_ZA, D_IN = 0, 1024, 2560, 4096, 5632, 6656, 7680
Q_SCALE = HEAD_DIM ** -0.5
NEG = -1e30

LANES = 128
GROUP_SLABS = D_GROUP // LANES
HALO = 16
CONV_HALO = 8
TM = 256
TM_BACK = 512
FFN_ROWS = 512
FFN_CHUNKS = 1
VMEM_LIMIT = 62 * 1024 * 1024
CACHE_BLOCK_BYTES = 4 * 1024 * 1024
N_SATTN_IN, N_SATTN_OUT = 8, 4


def _const_spec(shape):
    nd = len(shape)
    return pl.BlockSpec(shape, lambda *_: (0,) * nd, pipeline_mode=pl.Buffered(1))


def _rms(x, g):
    return x * lax.rsqrt(jnp.mean(x * x, axis=-1, keepdims=True) + EPS) * g


def _sigmoid(x):
    return 1.0 / (1.0 + jnp.exp(-x))


def _gelu_tanh(x):
    return 0.5 * x * (1.0 + jnp.tanh(0.7978845608028654 * (x + 0.044715 * (x * x * x))))


def _dot(a, b):
    return jnp.dot(a, b, preferred_element_type=F32)


def _params(*semantics):
    return pltpu.CompilerParams(dimension_semantics=semantics, vmem_limit_bytes=VMEM_LIMIT)


def _ada_kernel(c_ref, w_ref, b_ref, o_ref):
    c = c_ref[...]
    s = c * _sigmoid(c)
    o_ref[...] = _dot(s.astype(BF16), w_ref[...].astype(BF16)) + b_ref[...]


def _ada(c_all, w_ada, b_ada):
    m = c_all.shape[0]
    n = w_ada.shape[1]
    tn = 1536
    return pl.pallas_call(
        _ada_kernel,
        grid=(n // tn,),
        in_specs=[pl.BlockSpec((m, D_MODEL), lambda j: (0, 0)),
                  pl.BlockSpec((D_MODEL, tn), lambda j: (0, j)),
                  pl.BlockSpec((1, tn), lambda j: (0, j))],
        out_specs=pl.BlockSpec((m, tn), lambda j: (0, j)),
        out_shape=jax.ShapeDtypeStruct((m, n), F32),
        compiler_params=_params("arbitrary"),
        name="ada",
    )(c_all, w_ada, b_ada)


N_FRONT_IN, N_FRONT_OUT = 7, 18


def _front_kernel(*refs, cache_dils):
    n_in = N_FRONT_IN + N_SATTN_IN * len(cache_dils)
    x_ref, mod_ref, g_ref, win_ref, wmap_ref, pscale_ref, wpp_ref = refs[:N_FRONT_IN]
    (gp_ref, sg_ref, q0, k0, v0, q1, k1, v1, q2, k2, v2,
     kf0, vf0, kf1, vf1, kf2, vf2, utail_ref) = refs[n_in:n_in + N_FRONT_OUT]
    ubuf, slabs = refs[-2:]
    i = pl.program_id(1)
    tm = x_ref.shape[1]

    @pl.when((pl.program_id(0) == 0) & (i == 0))
    def _():
        ubuf[tm:tm + HALO, :] = jnp.zeros((HALO, D_MODEL), F32)

    hosted = []
    for c, cache_dil in enumerate(cache_dils):
        c_in = refs[N_FRONT_IN + c * N_SATTN_IN:N_FRONT_IN + (c + 1) * N_SATTN_IN]
        c_out = refs[n_in + N_FRONT_OUT + c * N_SATTN_OUT:n_in + N_FRONT_OUT + (c + 1) * N_SATTN_OUT]
        hosted.append((range(c_in[-1].shape[0]), _cache_update(c_in, c_out, cache_dil)))
    for rows, (c_scores, _, _) in hosted:
        for b in rows:
            c_scores(b)

    x = x_ref[0]
    shift = mod_ref[0, 0:1, :]
    scale = mod_ref[0, 1:2, :]
    h = (_rms(x, g_ref[...]) * (1.0 + scale) + shift).astype(BF16)

    u = _dot(h, win_ref[:, COL_U:COL_Q])
    ubuf[0:HALO, :] = jnp.where(i > 0, ubuf[tm:tm + HALO, :], 0.0)
    ubuf[HALO:HALO + tm, :] = u
    utail_ref[0] = u[tm - HALO:, :]

    pos = i * tm + lax.broadcasted_iota(jnp.int32, (tm, 1), 0)
    pooled = []
    for g, win in enumerate(POOL_WINDOWS):
        lo, hi = g * POOL_GROUP, (g + 1) * POOL_GROUP
        wsum = ubuf[HALO:HALO + tm, lo:hi]
        for j in range(1, win):
            wsum = wsum + ubuf[HALO - j:HALO - j + tm, lo:hi]
        count = jnp.minimum(pos + 1, win).astype(F32)
        pooled.append((wsum / count - ubuf[HALO:HALO + tm, lo:hi]).astype(BF16))

    sg_ref[0] = _sigmoid(_dot(h, win_ref[:, COL_ZA:D_IN])).astype(BF16)
    pool_gate = _sigmoid(_dot(h, win_ref[:, COL_ZP:COL_ZA]))

    def emit(val, refs, frefs):
        for j in range(D_ATT // LANES):
            slabs[j] = val[:, j * LANES:(j + 1) * LANES]
        for g, (_, dil) in enumerate(ATT_GROUPS):
            if dil == 1:
                refs[g][0, 0] = val[:, g * D_GROUP:(g + 1) * D_GROUP].astype(BF16)
                continue
            for r in range(dil):
                for jj in range(GROUP_SLABS):
                    piece = slabs[g * GROUP_SLABS + jj, pl.ds(r, tm // dil, stride=dil), :]
                    refs[g][0, r, :, jj * LANES:(jj + 1) * LANES] = piece.astype(BF16)
        for g, fref in enumerate(frefs):
            rows = fref.shape[2]
            fref[0] = val[tm - rows:, g * D_GROUP:(g + 1) * D_GROUP].T

    emit(_dot(h, win_ref[:, COL_Q:COL_K]) * Q_SCALE, (q0, q1, q2), ())
    for rows, (_, c_softmax, c_values) in hosted:
        for b in rows:
            c_softmax(b)
        for b in rows:
            c_values(b)
    emit(_dot(h, win_ref[:, COL_K:COL_V]), (k0, k1, k2), (kf0, kf1, kf2))
    emit(_dot(h, win_ref[:, COL_V:COL_ZP]), (v0, v1, v2), (vf0, vf1, vf2))

    mixed = [_dot(p, wmap_ref[g]) for g, p in enumerate(pooled)]
    mixed = jnp.concatenate(mixed, axis=-1) * pscale_ref[...]
    y_pool = _dot(mixed.astype(BF16), wpp_ref[...])
    gp_ref[0] = (pool_gate * y_pool).astype(BF16)


def _hosted_caches(caches, steps_per_batch, n_steps):
    args, ins, outs, shapes, dils = [], [], [], [], []
    for q, kn, vn, kt, vt, dil in caches:
        bb = kt.shape[0] // n_steps
        assert bb * n_steps == kt.shape[0]
        ops = _cache_operands(q, kn, vn, kt, vt, bb, lambda bi, i: bi * steps_per_batch + i)
        for acc, new in zip((args, ins, outs, shapes), ops):
            acc += new
        dils.append(dil)
    return args, ins, outs, shapes, tuple(dils)


def _front(x, mod, g_pre, w_in, w_map, pool_scale, w_pp, caches=()):
    b, t, d = x.shape
    tm = TM
    nblk = t // tm
    c_args, c_ins, c_outs, c_shapes, cache_dils = _hosted_caches(caches, nblk, b * nblk)
    row = lambda shape: pl.BlockSpec(shape, lambda bi, i: (bi, i, 0))
    in_specs = [row((1, tm, d)),
                pl.BlockSpec((1, 6, d), lambda bi, i: (bi, 0, 0)),
                _const_spec((1, d)), _const_spec(w_in.shape), _const_spec(w_map.shape),
                _const_spec((1, d)), _const_spec(w_pp.shape)]
    out_shape = [jax.ShapeDtypeStruct((b, t, d), BF16), jax.ShapeDtypeStruct((b, t, d), BF16)]
    out_specs = [row((1, tm, d)), row((1, tm, d))]
    for _, dil in ATT_GROUPS:
        for _ in range(3):
            out_shape.append(jax.ShapeDtypeStruct((b, dil, t // dil, D_GROUP), BF16))
            out_specs.append(pl.BlockSpec((1, dil, tm // dil, D_GROUP), lambda bi, i: (bi, 0, i, 0)))
    for win, _ in ATT_GROUPS:
        keep = min(win, t)
        rows = min(keep, tm)
        first = nblk - keep // rows
        for _ in range(2):
            out_shape.append(jax.ShapeDtypeStruct((b, D_GROUP, keep), F32))
            out_specs.append(pl.BlockSpec(
                (1, D_GROUP, rows), lambda bi, i, first=first: (bi, 0, jnp.maximum(i - first, 0))))
    out_shape.append(jax.ShapeDtypeStruct((b, HALO, d), F32))
    out_specs.append(pl.BlockSpec((1, HALO, d), lambda bi, i: (bi, 0, 0)))
    assert (len(in_specs), len(out_specs)) == (N_FRONT_IN, N_FRONT_OUT)
    return pl.pallas_call(
        functools.partial(_front_kernel, cache_dils=cache_dils),
        grid=(b, nblk),
        in_specs=in_specs + c_ins, out_specs=out_specs + c_outs, out_shape=out_shape + c_shapes,
        scratch_shapes=[pltpu.VMEM((HALO + tm, d), F32),
                        pltpu.VMEM((D_ATT // LANES, tm, LANES), F32)],
        compiler_params=_params("arbitrary", "arbitrary"),
        name="front",
    )(x, mod, g_pre, w_in, w_map, pool_scale, w_pp, *c_args)


def _attn_unit_tasks(q_ref, kp_ref, kc_ref, vp_ref, vc_ref, o_ref, lse_ref, n, r, dil, has_prev):
    s2 = 2 * SPAN
    lane = lax.broadcasted_iota(jnp.int32, (SPAN, LANES), 1)
    low = lane < HEAD_DIM
    rows = pl.ds(r, SPAN, stride=dil) if dil > 1 else slice(None)
    state = {"lse": jnp.zeros((SPAN, LANES), F32)}

    def pair(j):
        n_keys = s2 if has_prev else SPAN
        row = lax.broadcasted_iota(jnp.int32, (s2, n_keys), 0) % SPAN
        col = lax.broadcasted_iota(jnp.int32, (s2, n_keys), 1)
        sl = slice(j * LANES, (j + 1) * LANES)
        if has_prev:
            first_row = row + jnp.where(n > 0, 0, s2)
            mask = jnp.where(col < SPAN, col - first_row, row - (col - SPAN)) >= 0
            kk = jnp.concatenate([kp_ref[:, sl], kc_ref[:, sl]], axis=0)
            vv = jnp.concatenate([vp_ref[:, sl], vc_ref[:, sl]], axis=0)
        else:
            mask = col <= row
            kk, vv = kc_ref[:, sl], vc_ref[:, sl]
        q2 = q_ref[:, sl]
        zero = jnp.zeros_like(q2)
        qs = jnp.concatenate([jnp.where(low, q2, zero), jnp.where(low, zero, q2)], axis=0)
        s = lax.dot_general(qs, kk, (((1,), (1,)), ((), ())), preferred_element_type=F32)
        s = jnp.where(mask, s, NEG)
        m = jnp.max(s, axis=-1, keepdims=True)
        p = jnp.exp(s - m)
        den = jnp.sum(p, axis=-1, keepdims=True)
        o2 = _dot(p.astype(BF16), vv) / den
        lse2 = m + jnp.log(den)
        o_ref[j, rows, :] = jnp.where(low, o2[:SPAN], o2[SPAN:])
        tile = jnp.where(lane == 2 * j, lse2[:SPAN], state["lse"])
        state["lse"] = jnp.where(lane == 2 * j + 1, lse2[SPAN:], tile)

    def finish():
        lse_ref[rows, :] = state["lse"]

    return [functools.partial(pair, j) for j in range(N_SLOTS // 2)] + [finish]


def _attn_unit_specs(q, dil):
    b, _, n_sub, _ = q.shape
    nb = n_sub // SPAN
    t = n_sub * dil
    unit = lambda i: (i // (nb * dil), (i // dil) % nb, i % dil)

    def cur(i):
        bi, n, r = unit(i)
        return bi, r, n, 0

    def prev(i):
        bi, n, r = unit(i)
        return bi, r, jnp.maximum(n - 1, 0), 0

    blk = lambda f: pl.BlockSpec((None, None, SPAN, D_GROUP), f)
    in_specs = [blk(cur), blk(prev), blk(cur), blk(prev), blk(cur)]
    out_specs = [pl.BlockSpec((None, GROUP_SLABS, SPAN * dil, LANES), lambda i: (unit(i)[0], 0, unit(i)[1], 0)),
                 pl.BlockSpec((None, SPAN * dil, LANES), lambda i: (unit(i)[0], unit(i)[1], 0))]
    out_shape = [jax.ShapeDtypeStruct((b, GROUP_SLABS, t, LANES), F32),
                 jax.ShapeDtypeStruct((b, t, LANES), F32)]
    return b * nb * dil, nb, in_specs, out_specs, out_shape


def _merge_groups(o_list, lse_list):
    r = lax.broadcasted_iota(jnp.int32, (LANES, D_GROUP), 0)
    c = lax.broadcasted_iota(jnp.int32, (LANES, D_GROUP), 1)
    expand = jnp.where(c // HEAD_DIM == r, 1.0, 0.0).astype(BF16)
    top = jnp.maximum(jnp.maximum(lse_list[0], lse_list[1]), lse_list[2])
    e = [jnp.exp(l - top) for l in lse_list]
    den = e[0] + e[1] + e[2]
    spread = []
    for eg in e[:-1]:
        w = eg / den
        hi = w.astype(BF16)
        lo = (w - hi.astype(F32)).astype(BF16)
        spread.append(_dot(hi, expand) + _dot(lo, expand))
    spread.append(1.0 - spread[0] - spread[1])
    return spread[0] * o_list[0] + spread[1] * o_list[1] + spread[2] * o_list[2]


def _gated_ffn(project, conv_taps, wconv_ref, bconv_ref, wdown_ref, row_groups, n_chunks):
    fc = D_FF // n_chunks
    cols = lambda c: (slice(c * fc, (c + 1) * fc), slice(D_FF + c * fc, D_FF + (c + 1) * fc))

    def up(r, c):
        for cs in cols(c):
            project(r, cs)

    def gate(r, c):
        halves = []
        for cs in cols(c):
            hc = bconv_ref[:, cs]
            for j in range(CONV_W):
                hc = hc + conv_taps(j, r, cs) * wconv_ref[j:j + 1, cs]
            halves.append(hc)
        return (_gelu_tanh(halves[0]) * halves[1]).astype(BF16)

    def down(acc, gated, c):
        part = _dot(gated, wdown_ref[c * fc:(c + 1) * fc, :])
        return part if acc is None else acc + part

    for c in range(n_chunks):
        up(0, c)
    out, gated, acc_prev = [], {}, None
    for r in range(row_groups):
        for c in range(n_chunks):
            if r + 1 < row_groups:
                up(r + 1, c)
            if r > 0:
                acc_prev = down(acc_prev, gated.pop((r - 1, c)), c)
            gated[r, c] = gate(r, c)
        if r > 0:
            out.append(acc_prev)
        acc_prev = None
    for c in range(n_chunks):
        acc_prev = down(acc_prev, gated.pop((row_groups - 1, c)), c)
    out.append(acc_prev)
    return out[0] if row_groups == 1 else jnp.concatenate(out, axis=0)


def _back_kernel(x_ref, mod_ref, gp_ref, sg_ref, o0, o1, o2, l0, l1, l2,
                 wpa_ref, wout_ref, wup_ref, wconv_ref, bconv_ref, wdown_ref,
                 gpost_ref, gpre_ref, gffn_ref, y_ref, ctail_ref, *hbufs):
    i = pl.program_id(1)
    n = hbufs[0].shape[0] - CONV_HALO

    @pl.when((pl.program_id(0) == 0) & (i == 0))
    def _():
        hbufs[-1][n:n + CONV_HALO, :] = jnp.zeros((CONV_HALO, 2 * D_FF), F32)

    mod = lambda r: mod_ref[0, r:r + 1, :]
    o_list = [jnp.concatenate([ref[j] for j in range(GROUP_SLABS)], axis=-1) for ref in (o0, o1, o2)]
    o = _merge_groups(o_list, [l0[0], l1[0], l2[0]])
    y_att = _dot(o.astype(BF16), wpa_ref[...])
    mix = gp_ref[0].astype(F32) + sg_ref[0].astype(F32) * y_att
    x1 = x_ref[0] + mod(2) * _rms(_dot(mix.astype(BF16), wout_ref[...]), gpost_ref[...])
    h2 = (_rms(x1, gpre_ref[...]) * (1.0 + mod(4)) + mod(3)).astype(BF16)

    hbufs[0][0:CONV_HALO, :] = jnp.where(i > 0, hbufs[-1][n:n + CONV_HALO, :], 0.0)

    def project(r, cs):
        up = _dot(h2[r * n:(r + 1) * n], wup_ref[:, cs])
        hbufs[r][CONV_HALO:CONV_HALO + n, cs] = up
        if r + 1 < len(hbufs):
            hbufs[r + 1][0:CONV_HALO, cs] = up[n - CONV_HALO:]

    def taps(j, r, cs):
        off = CONV_HALO - (CONV_W - 1) + j
        return hbufs[r][off:off + n, cs]

    ffn = _gated_ffn(project, taps, wconv_ref, bconv_ref, wdown_ref, len(hbufs), FFN_CHUNKS)
    y_ref[0] = x1 + mod(5) * _rms(ffn, gffn_ref[...])
    ctail_ref[0] = hbufs[-1][n:n + CONV_HALO, :]


def _back(x, mod, gp, sg, o_list, lse_list, w_pa, w_out, w_up, w_conv, b_conv, w_down,
          g_post, g_pre, g_ffn):
    b, t, d = x.shape
    tm = TM_BACK
    row = lambda shape: pl.BlockSpec(shape, lambda bi, i: (bi, i, 0))
    slab = pl.BlockSpec((None, GROUP_SLABS, tm, LANES), lambda bi, i: (bi, 0, i, 0))
    in_specs = ([row((1, tm, d)), pl.BlockSpec((1, 6, d), lambda bi, i: (bi, 0, 0)),
                 row((1, tm, d)), row((1, tm, d))]
                + [slab] * N_GROUPS + [row((1, tm, LANES))] * N_GROUPS
                + [_const_spec(w.shape) for w in (w_pa, w_out, w_up, w_conv, b_conv, w_down)]
                + [_const_spec((1, d))] * 3)
    return pl.pallas_call(
        _back_kernel,
        grid=(b, t // tm),
        in_specs=in_specs,
        out_specs=[row((1, tm, d)), pl.BlockSpec((1, CONV_HALO, 2 * D_FF), lambda bi, i: (bi, 0, 0))],
        out_shape=[jax.ShapeDtypeStruct((b, t, d), F32),
                   jax.ShapeDtypeStruct((b, CONV_HALO, 2 * D_FF), F32)],
        scratch_shapes=[pltpu.VMEM((CONV_HALO + FFN_ROWS, 2 * D_FF), F32)] * (tm // FFN_ROWS),
        compiler_params=_params("arbitrary", "arbitrary"),
        name="back",
    )(x, mod, gp, sg, *o_list, *lse_list, w_pa, w_out, w_up, w_conv, b_conv, w_down,
      g_post, g_pre, g_ffn)


def _sfront_kernel(x_ref, shift_ref, scale_ref, g_ref, w_ref, o_ref, h_scr):
    @pl.when(pl.program_id(0) == 0)
    def _():
        h = _rms(x_ref[...], g_ref[...]) * (1.0 + scale_ref[...]) + shift_ref[...]
        h_scr[...] = h.astype(BF16)

    o_ref[...] = _dot(h_scr[...], w_ref[...])


def _sfront(x, mod, g_pre, w_in):
    m, d = x.shape
    n = w_in.shape[1]
    tn = 1536
    return pl.pallas_call(
        _sfront_kernel,
        grid=(n // tn,),
        in_specs=[pl.BlockSpec((m, d), lambda j: (0, 0)),
                  pl.BlockSpec((m, d), lambda j: (0, 0)),
                  pl.BlockSpec((m, d), lambda j: (0, 1)),
                  pl.BlockSpec((1, d), lambda j: (0, 0)),
                  pl.BlockSpec((d, tn), lambda j: (0, j))],
        out_specs=pl.BlockSpec((m, tn), lambda j: (0, j)),
        out_shape=jax.ShapeDtypeStruct((m, n), F32),
        scratch_shapes=[pltpu.VMEM((m, d), BF16)],
        compiler_params=_params("arbitrary"),
        name="sfront",
    )(x, mod, mod, g_pre, w_in)


def _cache_update(in_refs, out_refs, dil):
    qrow_ref, q_ref, kn_ref, vn_ref, knt_ref, vnt_ref, kt_ref, vt_ref = in_refs
    o_ref, lse_ref, kto_ref, vto_ref = out_refs
    bb, n_slots, dh, length = kt_ref.shape
    lane = lax.broadcasted_iota(jnp.int32, (1, length), 1)
    attended = (lane & (dil - 1)) == 0
    last = lane == length - 1
    slot = lax.broadcasted_iota(jnp.int32, (n_slots, 1), 0)
    state = [dict(sc=jnp.zeros((n_slots, length), F32), o=jnp.zeros((n_slots, dh), F32)) for _ in range(bb)]

    def score_slot(b, s):
        kt = kt_ref[b, s]
        state[b]["sc"] = state[b]["sc"] + _dot(qrow_ref[b, s], kt.astype(BF16))
        kto_ref[b, s] = jnp.where(last, knt_ref[b, :, s:s + 1], pltpu.roll(kt, length - 1, 1))

    def softmax(b):
        sc_new = jnp.sum(kn_ref[b] * q_ref[b], axis=-1, keepdims=True) * Q_SCALE
        sc = jnp.where(attended, state[b]["sc"], NEG)
        m = jnp.maximum(jnp.max(sc, axis=-1, keepdims=True), sc_new)
        p = jnp.exp(sc - m)
        p_new = jnp.exp(sc_new - m)
        den = jnp.sum(p, axis=-1, keepdims=True) + p_new
        lse_ref[b] = m + jnp.log(den)
        state[b].update(p=p.astype(BF16), p_new=p_new, inv=1.0 / den)

    def value_slot(b, s):
        vt = vt_ref[b, s]
        pv = lax.dot_general(state[b]["p"], vt.astype(BF16), (((1,), (1,)), ((), ())),
                             preferred_element_type=F32)
        state[b]["o"] = jnp.where(slot == s, pv, state[b]["o"])
        vto_ref[b, s] = jnp.where(last, vnt_ref[b, :, s:s + 1], pltpu.roll(vt, length - 1, 1))

    def scores(b):
        for s in range(n_slots):
            score_slot(b, s)

    def values(b):
        for s in range(n_slots):
            value_slot(b, s)
        o_ref[b] = (state[b]["o"] + vn_ref[b] * state[b]["p_new"]) * state[b]["inv"]

    return scores, softmax, values


def _cache_operands(q, kn, vn, kt, vt, bb, block):
    b, n_slots, dh, length = kt.shape
    own_row = jnp.eye(n_slots, dtype=F32)[None, :, :, None]
    qrow = (own_row * (q * Q_SCALE)[:, :, None, :]).astype(BF16)
    spec = lambda *shape: pl.BlockSpec((bb,) + shape, lambda *ids: (block(*ids),) + (0,) * len(shape))
    new, column, cache = spec(n_slots, dh), spec(dh, n_slots), spec(n_slots, dh, length)
    in_specs = [spec(n_slots, n_slots, dh), new, new, new, column, column, cache, cache]
    assert len(in_specs) == N_SATTN_IN
    out_specs = [new, spec(n_slots, 1), cache, cache]
    out_shape = [jax.ShapeDtypeStruct((b, n_slots, dh), F32),
                 jax.ShapeDtypeStruct((b, n_slots, 1), F32),
                 jax.ShapeDtypeStruct(kt.shape, F32), jax.ShapeDtypeStruct(vt.shape, F32)]
    args = [qrow, q, kn, vn, kn.transpose(0, 2, 1), vn.transpose(0, 2, 1), kt, vt]
    return args, in_specs, out_specs, out_shape


def _sattn_kernel(*refs, dil, units):
    n_in = N_SATTN_IN + 5 * len(units)
    scores, softmax, values = _cache_update(refs[:N_SATTN_IN], refs[n_in:n_in + N_SATTN_OUT], dil)
    for b in range(refs[N_SATTN_IN - 1].shape[0]):
        scores(b)
        softmax(b)
        values(b)
    step = pl.program_id(0)
    for g, (nb, unit_dil) in enumerate(units):
        ins = refs[N_SATTN_IN + 5 * g:N_SATTN_IN + 5 * g + 5]
        outs = refs[n_in + N_SATTN_OUT + 2 * g:n_in + N_SATTN_OUT + 2 * g + 2]
        for task in _attn_unit_tasks(*ins, *outs, (step // unit_dil) % nb, step % unit_dil, unit_dil, nb > 1):
            task()


def _sattn(q, kn, vn, kt, vt, dil, prompt=()):
    b, n_slots, dh, length = kt.shape
    bb = max(1, CACHE_BLOCK_BYTES // (n_slots * dh * length * 4))
    steps = b // bb
    args, in_specs, out_specs, out_shape = _cache_operands(q, kn, vn, kt, vt, bb, lambda i: i)
    units = []
    for pq, pk, pv, pdil in prompt:
        n_units, nb, ins, outs, shapes = _attn_unit_specs(pq, pdil)
        assert n_units == steps, (n_units, steps)
        units.append((nb, pdil))
        in_specs += ins
        out_specs += outs
        out_shape += shapes
        args += [pq, pk, pk, pv, pv]
    return pl.pallas_call(
        functools.partial(_sattn_kernel, dil=dil, units=tuple(units)),
        grid=(steps,),
        in_specs=in_specs, out_specs=out_specs, out_shape=out_shape,
        compiler_params=_params("arbitrary"),
        name="sattn",
    )(*args)


def _pool_state_kernel(hist_ref, u_ref, o_ref):
    newest = pl.program_id(0) == POOL_HIST - 1
    o_ref[0] = jnp.where(newest, u_ref[...], hist_ref[0])


def _pool_state(hist_t, proj):
    steps, m, d = hist_t.shape
    return pl.pallas_call(
        _pool_state_kernel,
        grid=(steps,),
        in_specs=[pl.BlockSpec((1, m, d), lambda j: (jnp.minimum(j + 1, steps - 1), 0, 0)),
                  pl.BlockSpec((m, d), lambda j: (0, COL_U // d))],
        out_specs=pl.BlockSpec((1, m, d), lambda j: (j, 0, 0)),
        out_shape=jax.ShapeDtypeStruct(hist_t.shape, F32),
        compiler_params=_params("arbitrary"),
        name="pool_state",
    )(hist_t, proj)


def _sback_kernel(x_ref, mod_ref, proj_ref, hist_ref, o0, o1, o2, l0, l1, l2, chist_ref,
                  wmap_ref, pscale_ref, wpp_ref, wpa_ref, wout_ref, wup_ref, wconv_ref, bconv_ref,
                  wdown_ref, gpost_ref, gpre_ref, gffn_ref, y_ref, hu_ref):
    m = x_ref.shape[0]
    mod = lambda r: mod_ref[:, r * D_MODEL:(r + 1) * D_MODEL]
    u = proj_ref[:, COL_U:COL_Q]
    mixed = []
    for g, win in enumerate(POOL_WINDOWS):
        lo, hi = g * POOL_GROUP, (g + 1) * POOL_GROUP
        wsum = u[:, lo:hi]
        for j in range(1, win):
            wsum = wsum + hist_ref[POOL_HIST - j, :, lo:hi]
        pooled = wsum / float(win) - u[:, lo:hi]
        mixed.append(_dot(pooled.astype(BF16), wmap_ref[g]))
    mixed = jnp.concatenate(mixed, axis=-1) * pscale_ref[...]
    y_pool = _dot(mixed.astype(BF16), wpp_ref[...])
    o = _merge_groups([o0[...], o1[...], o2[...]], [l0[...], l1[...], l2[...]])
    y_att = _dot(o.astype(BF16), wpa_ref[...])
    mix = (_sigmoid(proj_ref[:, COL_ZP:COL_ZA]) * y_pool + _sigmoid(proj_ref[:, COL_ZA:D_IN]) * y_att)
    x1 = x_ref[...] + mod(2) * _rms(_dot(mix.astype(BF16), wout_ref[...]), gpost_ref[...])
    h2 = (_rms(x1, gpre_ref[...]) * (1.0 + mod(4)) + mod(3)).astype(BF16)

    def project(r, cs):
        hu_ref[:, cs] = _dot(h2, wup_ref[:, cs])

    def taps(j, r, cs):
        return hu_ref[:, cs] if j == CONV_W - 1 else chist_ref[j, :, cs]

    ffn = _gated_ffn(project, taps, wconv_ref, bconv_ref, wdown_ref, 1, 1)
    y_ref[...] = x1 + mod(5) * _rms(ffn, gffn_ref[...])


def _sback(x, mod, proj, hist_t, o_list, lse_list, chist_t, weights, gains):
    m, d = x.shape
    args = [x, mod, proj, hist_t, *o_list, *lse_list, chist_t, *weights, *gains]
    full = lambda shape: pl.BlockSpec(shape, lambda i: (0,) * len(shape))
    out_shapes = [(m, d), (m, 2 * D_FF)]
    return pl.pallas_call(
        _sback_kernel,
        grid=(1,),
        in_specs=[_const_spec(a.shape) for a in args],
        out_specs=[full(s) for s in out_shapes],
        out_shape=[jax.ShapeDtypeStruct(s, F32) for s in out_shapes],
        compiler_params=_params("arbitrary"),
        name="sback",
    )(*args)


def kernel(x_prompt, x_sample, c_prompt, c_sample, cache_k_w128, cache_v_w128, cache_k_w512, cache_v_w512,
           cache_k_w2048, cache_v_w2048, state_pool, state_conv, w_ada, b_ada, g_pre_mix, g_post_mix,
           g_pre_ffn, g_post_ffn, w_in, w_pool_map, pool_scale, w_proj_pool, w_proj_att, w_out, w_up,
           w_conv, b_conv, w_down):
    bp, tp, d = x_prompt.shape
    bs = x_sample.shape[0]
    (w_ada, b_ada, g_pre_mix, g_post_mix, g_pre_ffn, g_post_ffn, w_in, w_pool_map, pool_scale,
     w_proj_pool, w_proj_att, w_out, w_up, w_conv, b_conv, w_down) = (
        w[0] for w in (w_ada, b_ada, g_pre_mix, g_post_mix, g_pre_ffn, g_post_ffn, w_in, w_pool_map,
                       pool_scale, w_proj_pool, w_proj_att, w_out, w_up, w_conv, b_conv, w_down))
    caches = [(cache_k_w128[0], cache_v_w128[0]), (cache_k_w512[0], cache_v_w512[0]),
              (cache_k_w2048[0], cache_v_w2048[0])]
    g_pre_mix, g_post_mix, g_pre_ffn, g_post_ffn, pool_scale, b_conv, b_ada = (
        a.reshape(1, -1) for a in (g_pre_mix, g_post_mix, g_pre_ffn, g_post_ffn, pool_scale, b_conv, b_ada))
    w_in_b, w_map_b, w_pp_b, w_pa_b, w_out_b, w_up_b, w_down_b = (
        w.astype(BF16) for w in (w_in, w_pool_map, w_proj_pool, w_proj_att, w_out, w_up, w_down))

    mod = _ada(jnp.concatenate([c_prompt, c_sample], axis=0), w_ada, b_ada)
    mod_p = mod[:bp].reshape(bp, 6, d)
    mod_s = mod[bp:]

    proj_s = _sfront(x_sample[:, 0], mod_s, g_pre_mix, w_in_b)
    heads = lambda lo: proj_s[:, lo:lo + D_ATT].reshape(bs, N_GROUPS, N_SLOTS, HEAD_DIM).transpose(1, 0, 2, 3)
    q_s, k_s, v_s = heads(COL_Q), heads(COL_K), heads(COL_V)
    cache_ops = [(q_s[g], k_s[g], v_s[g], kc.transpose(0, 2, 3, 1), vc.transpose(0, 2, 3, 1), dil)
                 for g, ((_, dil), (kc, vc)) in enumerate(zip(ATT_GROUPS, caches))]

    front = _front(x_prompt, mod_p, g_pre_mix, w_in_b, w_map_b, pool_scale, w_pp_b, cache_ops[:2])
    gp, sg = front[0], front[1]
    qkv = front[2:11]
    kv_t = front[11:17]
    u_tail = front[17]
    prompt_units = [(qkv[3 * g], qkv[3 * g + 1], qkv[3 * g + 2], dil) for g, (_, dil) in enumerate(ATT_GROUPS)]

    res = _sattn(*cache_ops[2], prompt_units)
    o_list, lse_list = list(res[N_SATTN_OUT::2]), list(res[N_SATTN_OUT + 1::2])
    cache_res = [front[N_FRONT_OUT:N_FRONT_OUT + N_SATTN_OUT], front[N_FRONT_OUT + N_SATTN_OUT:],
                 res[:N_SATTN_OUT]]
    o_s, lse_s, new_caches = [], [], []
    for o_g, lse_g, kt_new, vt_new in cache_res:
        o_s.append(o_g.reshape(bs, D_GROUP))
        lse_s.append(jnp.pad(lse_g[:, :, 0], ((0, 0), (0, LANES - N_SLOTS))))
        new_caches += [kt_new.transpose(0, 3, 1, 2)[None], vt_new.transpose(0, 3, 1, 2)[None]]
    hist_t = state_pool[0].transpose(1, 0, 2)
    chist_t = state_conv[0].transpose(1, 0, 2)
    pool_state_t = _pool_state(hist_t, proj_s)
    y_s, hu_s = _sback(
        x_sample[:, 0], mod_s, proj_s, hist_t, o_s, lse_s, chist_t,
        (w_map_b, pool_scale, w_pp_b, w_pa_b, w_out_b, w_up_b, w_conv, b_conv, w_down_b),
        (g_post_mix, g_pre_ffn, g_post_ffn))
    conv_state_s = jnp.concatenate([state_conv[0][:, 1:], hu_s[:, None]], axis=1)

    y_p, conv_tail = _back(x_prompt, mod_p, gp, sg, o_list, lse_list, w_pa_b, w_out_b, w_up_b, w_conv,
                           b_conv, w_down_b, g_post_mix, g_pre_ffn, g_post_ffn)

    outs = [y_p, y_s[:, None]]
    for g in range(N_GROUPS):
        for j in range(2):
            f = kv_t[2 * g + j]
            outs.append(f.reshape(bp, N_SLOTS, HEAD_DIM, f.shape[2]).transpose(0, 3, 1, 2)[None])
            outs.append(new_caches[2 * g + j])
    outs += [u_tail[None, :, HALO - POOL_HIST:], pool_state_t.transpose(1, 0, 2)[None],
             conv_tail[None, :, CONV_HALO - (CONV_W - 1):], conv_state_s[None]]
    return tuple(outs)
```

```python
import functools

import jax
import jax.numpy as jnp
from jax import lax
from jax.experimental import pallas as pl
from jax.experimental.pallas import tpu as pltpu

F32 = jnp.float32
BF16 = jnp.bfloat16

D_MODEL = 1024
EPS = 1e-6
POOL_WINDOWS = (2, 4, 8, 16)
POOL_GROUP = D_MODEL // len(POOL_WINDOWS)
POOL_HIST = max(POOL_WINDOWS) - 1
ATT_GROUPS = ((128, 1), (512, 4), (2048, 16))
SPAN = 128
N_GROUPS = len(ATT_GROUPS)
N_SLOTS = 8
HEAD_DIM = 64
D_GROUP = N_SLOTS * HEAD_DIM
D_ATT = N_GROUPS * D_GROUP
D_FF = 2816
CONV_W = 3
COL_U, COL_Q, COL_K, COL_V, COL_ZP, COL_ZA, D_IN = 0, 1024, 2560, 4096, 5632, 6656, 7680
Q_SCALE = HEAD_DIM ** -0.5
NEG = -1e30

LANES = 128
GROUP_SLABS = D_GROUP // LANES
HALO = 16
CONV_HALO = 8
TM = 256
TM_BACK = 512
FFN_ROWS = 512
FFN_CHUNKS = 1
VMEM_LIMIT = 62 * 1024 * 1024
CACHE_BLOCK_BYTES = 4 * 1024 * 1024
N_SATTN_IN, N_SATTN_OUT = 8, 4


def _const_spec(shape):
    nd = len(shape)
    return pl.BlockSpec(shape, lambda *_: (0,) * nd, pipeline_mode=pl.Buffered(1))


def _rms(x, g):
    return x * lax.rsqrt(jnp.mean(x * x, axis=-1, keepdims=True) + EPS) * g


def _sigmoid(x):
    return 1.0 / (1.0 + jnp.exp(-x))


def _gelu_tanh(x):
    return 0.5 * x * (1.0 + jnp.tanh(0.7978845608028654 * (x + 0.044715 * (x * x * x))))


def _dot(a, b):
    return jnp.dot(a, b, preferred_element_type=F32)


def _params(*semantics):
    return pltpu.CompilerParams(dimension_semantics=semantics, vmem_limit_bytes=VMEM_LIMIT)


def _ada_kernel(c_ref, w_ref, b_ref, o_ref):
    c = c_ref[...]
    s = c * _sigmoid(c)
    o_ref[...] = _dot(s.astype(BF16), w_ref[...].astype(BF16)) + b_ref[...]


def _ada(c_all, w_ada, b_ada):
    m = c_all.shape[0]
    n = w_ada.shape[1]
    tn = 1536
    return pl.pallas_call(
        _ada_kernel,
        grid=(n // tn,),
        in_specs=[pl.BlockSpec((m, D_MODEL), lambda j: (0, 0)),
                  pl.BlockSpec((D_MODEL, tn), lambda j: (0, j)),
                  pl.BlockSpec((1, tn), lambda j: (0, j))],
        out_specs=pl.BlockSpec((m, tn), lambda j: (0, j)),
        out_shape=jax.ShapeDtypeStruct((m, n), F32),
        compiler_params=_params("arbitrary"),
        name="ada",
    )(c_all, w_ada, b_ada)


N_FRONT_IN, N_FRONT_OUT = 7, 18


def _front_kernel(*refs, cache_dils):
    n_in = N_FRONT_IN + N_SATTN_IN * len(cache_dils)
    x_ref, mod_ref, g_ref, win_ref, wmap_ref, pscale_ref, wpp_ref = refs[:N_FRONT_IN]
    (gp_ref, sg_ref, q0, k0, v0, q1, k1, v1, q2, k2, v2,
     kf0, vf0, kf1, vf1, kf2, vf2, utail_ref) = refs[n_in:n_in + N_FRONT_OUT]
    ubuf, slabs = refs[-2:]
    i = pl.program_id(1)
    tm = x_ref.shape[1]

    @pl.when((pl.program_id(0) == 0) & (i == 0))
    def _():
        ubuf[tm:tm + HALO, :] = jnp.zeros((HALO, D_MODEL), F32)

    hosted = []
    for c, cache_dil in enumerate(cache_dils):
        c_in = refs[N_FRONT_IN + c * N_SATTN_IN:N_FRONT_IN + (c + 1) * N_SATTN_IN]
        c_out = refs[n_in + N_FRONT_OUT + c * N_SATTN_OUT:n_in + N_FRONT_OUT + (c + 1) * N_SATTN_OUT]
        hosted.append((range(c_in[-1].shape[0]), _cache_update(c_in, c_out, cache_dil)))
    for rows, (c_scores, _, _) in hosted:
        for b in rows:
            c_scores(b)

    x = x_ref[0]
    shift = mod_ref[0, 0:1, :]
    scale = mod_ref[0, 1:2, :]
    h = (_rms(x, g_ref[...]) * (1.0 + scale) + shift).astype(BF16)

    u = _dot(h, win_ref[:, COL_U:COL_Q])
    ubuf[0:HALO, :] = jnp.where(i > 0, ubuf[tm:tm + HALO, :], 0.0)
    ubuf[HALO:HALO + tm, :] = u
    utail_ref[0] = u[tm - HALO:, :]

    pos = i * tm + lax.broadcasted_iota(jnp.int32, (tm, 1), 0)
    pooled = []
    for g, win in enumerate(POOL_WINDOWS):
        lo, hi = g * POOL_GROUP, (g + 1) * POOL_GROUP
        wsum = ubuf[HALO:HALO + tm, lo:hi]
        for j in range(1, win):
            wsum = wsum + ubuf[HALO - j:HALO - j + tm, lo:hi]
        count = jnp.minimum(pos + 1, win).astype(F32)
        pooled.append((wsum / count - ubuf[HALO:HALO + tm, lo:hi]).astype(BF16))

    sg_ref[0] = _sigmoid(_dot(h, win_ref[:, COL_ZA:D_IN])).astype(BF16)
    pool_gate = _sigmoid(_dot(h, win_ref[:, COL_ZP:COL_ZA]))

    def emit(val, refs, frefs):
        for j in range(D_ATT // LANES):
            slabs[j] = val[:, j * LANES:(j + 1) * LANES]
        for g, (_, dil) in enumerate(ATT_GROUPS):
            if dil == 1:
                refs[g][0, 0] = val[:, g * D_GROUP:(g + 1) * D_GROUP].astype(BF16)
                continue
            for r in range(dil):
                for jj in range(GROUP_SLABS):
                    piece = slabs[g * GROUP_SLABS + jj, pl.ds(r, tm // dil, stride=dil), :]
                    refs[g][0, r, :, jj * LANES:(jj + 1) * LANES] = piece.astype(BF16)
        for g, fref in enumerate(frefs):
            rows = fref.shape[2]
            fref[0] = val[tm - rows:, g * D_GROUP:(g + 1) * D_GROUP].T

    emit(_dot(h, win_ref[:, COL_Q:COL_K]) * Q_SCALE, (q0, q1, q2), ())
    for rows, (_, c_softmax, c_values) in hosted:
        for b in rows:
            c_softmax(b)
        for b in rows:
            c_values(b)
    emit(_dot(h, win_ref[:, COL_K:COL_V]), (k0, k1, k2), (kf0, kf1, kf2))
    emit(_dot(h, win_ref[:, COL_V:COL_ZP]), (v0, v1, v2), (vf0, vf1, vf2))

    mixed = [_dot(p, wmap_ref[g]) for g, p in enumerate(pooled)]
    mixed = jnp.concatenate(mixed, axis=-1) * pscale_ref[...]
    y_pool = _dot(mixed.astype(BF16), wpp_ref[...])
    gp_ref[0] = (pool_gate * y_pool).astype(BF16)


def _hosted_caches(caches, steps_per_batch, n_steps):
    args, ins, outs, shapes, dils = [], [], [], [], []
    for q, kn, vn, kt, vt, dil in caches:
        bb = kt.shape[0] // n_steps
        assert bb * n_steps == kt.shape[0]
        ops = _cache_operands(q, kn, vn, kt, vt, bb, lambda bi, i: bi * steps_per_batch + i)
        for acc, new in zip((args, ins, outs, shapes), ops):
            acc += new
        dils.append(dil)
    return args, ins, outs, shapes, tuple(dils)


def _front(x, mod, g_pre, w_in, w_map, pool_scale, w_pp, caches=()):
    b, t, d = x.shape
    tm = TM
    nblk = t // tm
    c_args, c_ins, c_outs, c_shapes, cache_dils = _hosted_caches(caches, nblk, b * nblk)
    row = lambda shape: pl.BlockSpec(shape, lambda bi, i: (bi, i, 0))
    in_specs = [row((1, tm, d)),
                pl.BlockSpec((1, 6, d), lambda bi, i: (bi, 0, 0)),
                _const_spec((1, d)), _const_spec(w_in.shape), _const_spec(w_map.shape),
                _const_spec((1, d)), _const_spec(w_pp.shape)]
    out_shape = [jax.ShapeDtypeStruct((b, t, d), BF16), jax.ShapeDtypeStruct((b, t, d), BF16)]
    out_specs = [row((1, tm, d)), row((1, tm, d))]
    for _, dil in ATT_GROUPS:
        for _ in range(3):
            out_shape.append(jax.ShapeDtypeStruct((b, dil, t // dil, D_GROUP), BF16))
            out_specs.append(pl.BlockSpec((1, dil, tm // dil, D_GROUP), lambda bi, i: (bi, 0, i, 0)))
    for win, _ in ATT_GROUPS:
        keep = min(win, t)
        rows = min(keep, tm)
        first = nblk - keep // rows
        for _ in range(2):
            out_shape.append(jax.ShapeDtypeStruct((b, D_GROUP, keep), F32))
            out_specs.append(pl.BlockSpec(
                (1, D_GROUP, rows), lambda bi, i, first=first: (bi, 0, jnp.maximum(i - first, 0))))
    out_shape.append(jax.ShapeDtypeStruct((b, HALO, d), F32))
    out_specs.append(pl.BlockSpec((1, HALO, d), lambda bi, i: (bi, 0, 0)))
    assert (len(in_specs), len(out_specs)) == (N_FRONT_IN, N_FRONT_OUT)
    return pl.pallas_call(
        functools.partial(_front_kernel, cache_dils=cache_dils),
        grid=(b, nblk),
        in_specs=in_specs + c_ins, out_specs=out_specs + c_outs, out_shape=out_shape + c_shapes,
        scratch_shapes=[pltpu.VMEM((HALO + tm, d), F32),
                        pltpu.VMEM((D_ATT // LANES, tm, LANES), F32)],
        compiler_params=_params("arbitrary", "arbitrary"),
        name="front",
    )(x, mod, g_pre, w_in, w_map, pool_scale, w_pp, *c_args)


def _attn_unit_tasks(q_ref, kp_ref, kc_ref, vp_ref, vc_ref, o_ref, lse_ref, n, r, dil, has_prev):
    s2 = 2 * SPAN
    lane = lax.broadcasted_iota(jnp.int32, (SPAN, LANES), 1)
    low = lane < HEAD_DIM
    rows = pl.ds(r, SPAN, stride=dil) if dil > 1 else slice(None)
    state = {"lse": jnp.zeros((SPAN, LANES), F32)}

    def pair(j):
        n_keys = s2 if has_prev else SPAN
        row = lax.broadcasted_iota(jnp.int32, (s2, n_keys), 0) % SPAN
        col = lax.broadcasted_iota(jnp.int32, (s2, n_keys), 1)
        sl = slice(j * LANES, (j + 1) * LANES)
        if has_prev:
            first_row = row + jnp.where(n > 0, 0, s2)
            mask = jnp.where(col < SPAN, col - first_row, row - (col - SPAN)) >= 0
            kk = jnp.concatenate([kp_ref[:, sl], kc_ref[:, sl]], axis=0)
            vv = jnp.concatenate([vp_ref[:, sl], vc_ref[:, sl]], axis=0)
        else:
            mask = col <= row
            kk, vv = kc_ref[:, sl], vc_ref[:, sl]
        q2 = q_ref[:, sl]
        zero = jnp.zeros_like(q2)
        qs = jnp.concatenate([jnp.where(low, q2, zero), jnp.where(low, zero, q2)], axis=0)
        s = lax.dot_general(qs, kk, (((1,), (1,)), ((), ())), preferred_element_type=F32)
        s = jnp.where(mask, s, NEG)
        m = jnp.max(s, axis=-1, keepdims=True)
        p = jnp.exp(s - m)
        den = jnp.sum(p, axis=-1, keepdims=True)
        o2 = _dot(p.astype(BF16), vv) / den
        lse2 = m + jnp.log(den)
        o_ref[j, rows, :] = jnp.where(low, o2[:SPAN], o2[SPAN:])
        tile = jnp.where(lane == 2 * j, lse2[:SPAN], state["lse"])
        state["lse"] = jnp.where(lane == 2 * j + 1, lse2[SPAN:], tile)

    def finish():
        lse_ref[rows, :] = state["lse"]

    return [functools.partial(pair, j) for j in range(N_SLOTS // 2)] + [finish]


def _attn_unit_specs(q, dil):
    b, _, n_sub, _ = q.shape
    nb = n_sub // SPAN
    t = n_sub * dil
    unit = lambda i: (i // (nb * dil), (i // dil) % nb, i % dil)

    def cur(i):
        bi, n, r = unit(i)
        return bi, r, n, 0

    def prev(i):
        bi, n, r = unit(i)
        return bi, r, jnp.maximum(n - 1, 0), 0

    blk = lambda f: pl.BlockSpec((None, None, SPAN, D_GROUP), f)
    in_specs = [blk(cur), blk(prev), blk(cur), blk(prev), blk(cur)]
    out_specs = [pl.BlockSpec((None, GROUP_SLABS, SPAN * dil, LANES), lambda i: (unit(i)[0], 0, unit(i)[1], 0)),
                 pl.BlockSpec((None, SPAN * dil, LANES), lambda i: (unit(i)[0], unit(i)[1], 0))]
    out_shape = [jax.ShapeDtypeStruct((b, GROUP_SLABS, t, LANES), F32),
                 jax.ShapeDtypeStruct((b, t, LANES), F32)]
    return b * nb * dil, nb, in_specs, out_specs, out_shape


def _merge_groups(o_list, lse_list):
    r = lax.broadcasted_iota(jnp.int32, (LANES, D_GROUP), 0)
    c = lax.broadcasted_iota(jnp.int32, (LANES, D_GROUP), 1)
    expand = jnp.where(c // HEAD_DIM == r, 1.0, 0.0).astype(BF16)
    top = jnp.maximum(jnp.maximum(lse_list[0], lse_list[1]), lse_list[2])
    e = [jnp.exp(l - top) for l in lse_list]
    den = e[0] + e[1] + e[2]
    spread = []
    for eg in e[:-1]:
        w = eg / den
        hi = w.astype(BF16)
        lo = (w - hi.astype(F32)).astype(BF16)
        spread.append(_dot(hi, expand) + _dot(lo, expand))
    spread.append(1.0 - spread[0] - spread[1])
    return spread[0] * o_list[0] + spread[1] * o_list[1] + spread[2] * o_list[2]


def _gated_ffn(project, conv_taps, wconv_ref, bconv_ref, wdown_ref, row_groups, n_chunks):
    fc = D_FF // n_chunks
    cols = lambda c: (slice(c * fc, (c + 1) * fc), slice(D_FF + c * fc, D_FF + (c + 1) * fc))

    def up(r, c):
        for cs in cols(c):
            project(r, cs)

    def gate(r, c):
        halves = []
        for cs in cols(c):
            hc = bconv_ref[:, cs]
            for j in range(CONV_W):
                hc = hc + conv_taps(j, r, cs) * wconv_ref[j:j + 1, cs]
            halves.append(hc)
        return (_gelu_tanh(halves[0]) * halves[1]).astype(BF16)

    def down(acc, gated, c):
        part = _dot(gated, wdown_ref[c * fc:(c + 1) * fc, :])
        return part if acc is None else acc + part

    for c in range(n_chunks):
        up(0, c)
    out, gated, acc_prev = [], {}, None
    for r in range(row_groups):
        for c in range(n_chunks):
            if r + 1 < row_groups:
                up(r + 1, c)
            if r > 0:
                acc_prev = down(acc_prev, gated.pop((r - 1, c)), c)
            gated[r, c] = gate(r, c)
        if r > 0:
            out.append(acc_prev)
        acc_prev = None
    for c in range(n_chunks):
        acc_prev = down(acc_prev, gated.pop((row_groups - 1, c)), c)
    out.append(acc_prev)
    return out[0] if row_groups == 1 else jnp.concatenate(out, axis=0)


def _back_kernel(x_ref, mod_ref, gp_ref, sg_ref, o0, o1, o2, l0, l1, l2,
                 wpa_ref, wout_ref, wup_ref, wconv_ref, bconv_ref, wdown_ref,
                 gpost_ref, gpre_ref, gffn_ref, y_ref, ctail_ref, *hbufs):
    i = pl.program_id(1)
    n = hbufs[0].shape[0] - CONV_HALO

    @pl.when((pl.program_id(0) == 0) & (i == 0))
    def _():
        hbufs[-1][n:n + CONV_HALO, :] = jnp.zeros((CONV_HALO, 2 * D_FF), F32)

    mod = lambda r: mod_ref[0, r:r + 1, :]
    o_list = [jnp.concatenate([ref[j] for j in range(GROUP_SLABS)], axis=-1) for ref in (o0, o1, o2)]
    o = _merge_groups(o_list, [l0[0], l1[0], l2[0]])
    y_att = _dot(o.astype(BF16), wpa_ref[...])
    mix = gp_ref[0].astype(F32) + sg_ref[0].astype(F32) * y_att
    x1 = x_ref[0] + mod(2) * _rms(_dot(mix.astype(BF16), wout_ref[...]), gpost_ref[...])
    h2 = (_rms(x1, gpre_ref[...]) * (1.0 + mod(4)) + mod(3)).astype(BF16)

    hbufs[0][0:CONV_HALO, :] = jnp.where(i > 0, hbufs[-1][n:n + CONV_HALO, :], 0.0)

    def project(r, cs):
        up = _dot(h2[r * n:(r + 1) * n], wup_ref[:, cs])
        hbufs[r][CONV_HALO:CONV_HALO + n, cs] = up
        if r + 1 < len(hbufs):
            hbufs[r + 1][0:CONV_HALO, cs] = up[n - CONV_HALO:]

    def taps(j, r, cs):
        off = CONV_HALO - (CONV_W - 1) + j
        return hbufs[r][off:off + n, cs]

    ffn = _gated_ffn(project, taps, wconv_ref, bconv_ref, wdown_ref, len(hbufs), FFN_CHUNKS)
    y_ref[0] = x1 + mod(5) * _rms(ffn, gffn_ref[...])
    ctail_ref[0] = hbufs[-1][n:n + CONV_HALO, :]


def _back(x, mod, gp, sg, o_list, lse_list, w_pa, w_out, w_up, w_conv, b_conv, w_down,
          g_post, g_pre, g_ffn):
    b, t, d = x.shape
    tm = TM_BACK
    row = lambda shape: pl.BlockSpec(shape, lambda bi, i: (bi, i, 0))
    slab = pl.BlockSpec((None, GROUP_SLABS, tm, LANES), lambda bi, i: (bi, 0, i, 0))
    in_specs = ([row((1, tm, d)), pl.BlockSpec((1, 6, d), lambda bi, i: (bi, 0, 0)),
                 row((1, tm, d)), row((1, tm, d))]
                + [slab] * N_GROUPS + [row((1, tm, LANES))] * N_GROUPS
                + [_const_spec(w.shape) for w in (w_pa, w_out, w_up, w_conv, b_conv, w_down)]
                + [_const_spec((1, d))] * 3)
    return pl.pallas_call(
        _back_kernel,
        grid=(b, t // tm),
        in_specs=in_specs,
        out_specs=[row((1, tm, d)), pl.BlockSpec((1, CONV_HALO, 2 * D_FF), lambda bi, i: (bi, 0, 0))],
        out_shape=[jax.ShapeDtypeStruct((b, t, d), F32),
                   jax.ShapeDtypeStruct((b, CONV_HALO, 2 * D_FF), F32)],
        scratch_shapes=[pltpu.VMEM((CONV_HALO + FFN_ROWS, 2 * D_FF), F32)] * (tm // FFN_ROWS),
        compiler_params=_params("arbitrary", "arbitrary"),
        name="back",
    )(x, mod, gp, sg, *o_list, *lse_list, w_pa, w_out, w_up, w_conv, b_conv, w_down,
      g_post, g_pre, g_ffn)


def _sfront_kernel(x_ref, shift_ref, scale_ref, g_ref, w_ref, o_ref, heads_ref, h_scr):
    j = pl.program_id(0)

    @pl.when(j == 0)
    def _():
        h = _rms(x_ref[...], g_ref[...]) * (1.0 + scale_ref[...]) + shift_ref[...]
        h_scr[...] = h.astype(BF16)

    val = _dot(h_scr[...], w_ref[...])
    o_ref[...] = val

    @pl.when((j >= COL_Q // D_GROUP) & (j < COL_ZP // D_GROUP))
    def _():
        for s in range(N_SLOTS):
            heads_ref[0, :, s, :] = val[:, s * HEAD_DIM:(s + 1) * HEAD_DIM]


def _sfront(x, mod, g_pre, w_in):
    m, d = x.shape
    n = w_in.shape[1]
    tn = D_GROUP
    first, n_heads = COL_Q // tn, 3 * N_GROUPS
    return pl.pallas_call(
        _sfront_kernel,
        grid=(n // tn,),
        in_specs=[pl.BlockSpec((m, d), lambda j: (0, 0)),
                  pl.BlockSpec((m, d), lambda j: (0, 0)),
                  pl.BlockSpec((m, d), lambda j: (0, 1)),
                  pl.BlockSpec((1, d), lambda j: (0, 0)),
                  pl.BlockSpec((d, tn), lambda j: (0, j))],
        out_specs=[pl.BlockSpec((m, tn), lambda j: (0, j)),
                   pl.BlockSpec((1, m, N_SLOTS, HEAD_DIM),
                                lambda j: (jnp.clip(j - first, 0, n_heads - 1), 0, 0, 0))],
        out_shape=[jax.ShapeDtypeStruct((m, n), F32),
                   jax.ShapeDtypeStruct((n_heads, m, N_SLOTS, HEAD_DIM), F32)],
        scratch_shapes=[pltpu.VMEM((m, d), BF16)],
        compiler_params=_params("arbitrary"),
        name="sfront",
    )(x, mod, mod, g_pre, w_in)


def _cache_update(in_refs, out_refs, dil):
    qrow_ref, q_ref, kn_ref, vn_ref, knt_ref, vnt_ref, kt_ref, vt_ref = in_refs
    o_ref, lse_ref, kto_ref, vto_ref = out_refs
    bb, n_slots, dh, length = kt_ref.shape
    lane = lax.broadcasted_iota(jnp.int32, (1, length), 1)
    attended = (lane & (dil - 1)) == 0
    last = lane == length - 1
    slot = lax.broadcasted_iota(jnp.int32, (n_slots, 1), 0)
    state = [dict(sc=jnp.zeros((n_slots, length), F32), o=jnp.zeros((n_slots, dh), F32)) for _ in range(bb)]

    def score_slot(b, s):
        kt = kt_ref[b, s]
        state[b]["sc"] = state[b]["sc"] + _dot(qrow_ref[b, s], kt.astype(BF16))
        kto_ref[b, s] = jnp.where(last, knt_ref[b, :, s:s + 1], pltpu.roll(kt, length - 1, 1))

    def softmax(b):
        sc_new = jnp.sum(kn_ref[b] * q_ref[b], axis=-1, keepdims=True) * Q_SCALE
        sc = jnp.where(attended, state[b]["sc"], NEG)
        m = jnp.maximum(jnp.max(sc, axis=-1, keepdims=True), sc_new)
        p = jnp.exp(sc - m)
        p_new = jnp.exp(sc_new - m)
        den = jnp.sum(p, axis=-1, keepdims=True) + p_new
        lse_ref[b] = m + jnp.log(den)
        state[b].update(p=p.astype(BF16), p_new=p_new, inv=1.0 / den)

    def value_slot(b, s):
        vt = vt_ref[b, s]
        pv = lax.dot_general(state[b]["p"], vt.astype(BF16), (((1,), (1,)), ((), ())),
                             preferred_element_type=F32)
        state[b]["o"] = jnp.where(slot == s, pv, state[b]["o"])
        vto_ref[b, s] = jnp.where(last, vnt_ref[b, :, s:s + 1], pltpu.roll(vt, length - 1, 1))

    def scores(b):
        for s in range(n_slots):
            score_slot(b, s)

    def values(b):
        for s in range(n_slots):
            value_slot(b, s)
        o_ref[b] = (state[b]["o"] + vn_ref[b] * state[b]["p_new"]) * state[b]["inv"]

    return scores, softmax, values


def _cache_operands(q, kn, vn, kt, vt, bb, block):
    b, n_slots, dh, length = kt.shape
    own_row = jnp.eye(n_slots, dtype=F32)[None, :, :, None]
    qrow = (own_row * (q * Q_SCALE)[:, :, None, :]).astype(BF16)
    spec = lambda *shape: pl.BlockSpec((bb,) + shape, lambda *ids: (block(*ids),) + (0,) * len(shape))
    new, column, cache = spec(n_slots, dh), spec(dh, n_slots), spec(n_slots, dh, length)
    in_specs = [spec(n_slots, n_slots, dh), new, new, new, column, column, cache, cache]
    assert len(in_specs) == N_SATTN_IN
    out_specs = [new, spec(n_slots, 1), cache, cache]
    out_shape = [jax.ShapeDtypeStruct((b, n_slots, dh), F32),
                 jax.ShapeDtypeStruct((b, n_slots, 1), F32),
                 jax.ShapeDtypeStruct(kt.shape, F32), jax.ShapeDtypeStruct(vt.shape, F32)]
    args = [qrow, q, kn, vn, kn.transpose(0, 2, 1), vn.transpose(0, 2, 1), kt, vt]
    return args, in_specs, out_specs, out_shape


def _sattn_kernel(*refs, dil, units):
    n_in = N_SATTN_IN + 5 * len(units)
    scores, softmax, values = _cache_update(refs[:N_SATTN_IN], refs[n_in:n_in + N_SATTN_OUT], dil)
    for b in range(refs[N_SATTN_IN - 1].shape[0]):
        scores(b)
        softmax(b)
        values(b)
    step = pl.program_id(0)
    for g, (nb, unit_dil) in enumerate(units):
        ins = refs[N_SATTN_IN + 5 * g:N_SATTN_IN + 5 * g + 5]
        outs = refs[n_in + N_SATTN_OUT + 2 * g:n_in + N_SATTN_OUT + 2 * g + 2]
        for task in _attn_unit_tasks(*ins, *outs, (step // unit_dil) % nb, step % unit_dil, unit_dil, nb > 1):
            task()


def _sattn(q, kn, vn, kt, vt, dil, prompt=()):
    b, n_slots, dh, length = kt.shape
    bb = max(1, CACHE_BLOCK_BYTES // (n_slots * dh * length * 4))
    steps = b // bb
    args, in_specs, out_specs, out_shape = _cache_operands(q, kn, vn, kt, vt, bb, lambda i: i)
    units = []
    for pq, pk, pv, pdil in prompt:
        n_units, nb, ins, outs, shapes = _attn_unit_specs(pq, pdil)
        assert n_units == steps, (n_units, steps)
        units.append((nb, pdil))
        in_specs += ins
        out_specs += outs
        out_shape += shapes
        args += [pq, pk, pk, pv, pv]
    return pl.pallas_call(
        functools.partial(_sattn_kernel, dil=dil, units=tuple(units)),
        grid=(steps,),
        in_specs=in_specs, out_specs=out_specs, out_shape=out_shape,
        compiler_params=_params("arbitrary"),
        name="sattn",
    )(*args)


def _pool_state_kernel(hist_ref, u_ref, o_ref):
    newest = pl.program_id(0) == POOL_HIST - 1
    o_ref[0] = jnp.where(newest, u_ref[...], hist_ref[0])


def _pool_state(hist_t, proj):
    steps, m, d = hist_t.shape
    return pl.pallas_call(
        _pool_state_kernel,
        grid=(steps,),
        in_specs=[pl.BlockSpec((1, m, d), lambda j: (jnp.minimum(j + 1, steps - 1), 0, 0)),
                  pl.BlockSpec((m, d), lambda j: (0, COL_U // d))],
        out_specs=pl.BlockSpec((1, m, d), lambda j: (j, 0, 0)),
        out_shape=jax.ShapeDtypeStruct(hist_t.shape, F32),
        compiler_params=_params("arbitrary"),
        name="pool_state",
    )(hist_t, proj)


def _sback_kernel(x_ref, mod_ref, proj_ref, hist_ref, o0, o1, o2, l0, l1, l2, chist_ref,
                  wmap_ref, pscale_ref, wpp_ref, wpa_ref, wout_ref, wup_ref, wconv_ref, bconv_ref,
                  wdown_ref, gpost_ref, gpre_ref, gffn_ref, y_ref, hu_ref):
    m = x_ref.shape[0]
    mod = lambda r: mod_ref[:, r * D_MODEL:(r + 1) * D_MODEL]
    u = proj_ref[:, COL_U:COL_Q]
    mixed = []
    for g, win in enumerate(POOL_WINDOWS):
        lo, hi = g * POOL_GROUP, (g + 1) * POOL_GROUP
        wsum = u[:, lo:hi]
        for j in range(1, win):
            wsum = wsum + hist_ref[POOL_HIST - j, :, lo:hi]
        pooled = wsum / float(win) - u[:, lo:hi]
        mixed.append(_dot(pooled.astype(BF16), wmap_ref[g]))
    mixed = jnp.concatenate(mixed, axis=-1) * pscale_ref[...]
    y_pool = _dot(mixed.astype(BF16), wpp_ref[...])
    o = _merge_groups([o0[...], o1[...], o2[...]], [l0[...], l1[...], l2[...]])
    y_att = _dot(o.astype(BF16), wpa_ref[...])
    mix = (_sigmoid(proj_ref[:, COL_ZP:COL_ZA]) * y_pool + _sigmoid(proj_ref[:, COL_ZA:D_IN]) * y_att)
    x1 = x_ref[...] + mod(2) * _rms(_dot(mix.astype(BF16), wout_ref[...]), gpost_ref[...])
    h2 = (_rms(x1, gpre_ref[...]) * (1.0 + mod(4)) + mod(3)).astype(BF16)

    def project(r, cs):
        hu_ref[:, cs] = _dot(h2, wup_ref[:, cs])

    def taps(j, r, cs):
        return hu_ref[:, cs] if j == CONV_W - 1 else chist_ref[j, :, cs]

    ffn = _gated_ffn(project, taps, wconv_ref, bconv_ref, wdown_ref, 1, 1)
    y_ref[...] = x1 + mod(5) * _rms(ffn, gffn_ref[...])


def _sback(x, mod, proj, hist_t, o_list, lse_list, chist_t, weights, gains):
    m, d = x.shape
    args = [x, mod, proj, hist_t, *o_list, *lse_list, chist_t, *weights, *gains]
    full = lambda shape: pl.BlockSpec(shape, lambda i: (0,) * len(shape))
    out_shapes = [(m, d), (m, 2 * D_FF)]
    return pl.pallas_call(
        _sback_kernel,
        grid=(1,),
        in_specs=[_const_spec(a.shape) for a in args],
        out_specs=[full(s) for s in out_shapes],
        out_shape=[jax.ShapeDtypeStruct(s, F32) for s in out_shapes],
        compiler_params=_params("arbitrary"),
        name="sback",
    )(*args)


def kernel(x_prompt, x_sample, c_prompt, c_sample, cache_k_w128, cache_v_w128, cache_k_w512, cache_v_w512,
           cache_k_w2048, cache_v_w2048, state_pool, state_conv, w_ada, b_ada, g_pre_mix, g_post_mix,
           g_pre_ffn, g_post_ffn, w_in, w_pool_map, pool_scale, w_proj_pool, w_proj_att, w_out, w_up,
           w_conv, b_conv, w_down):
    bp, tp, d = x_prompt.shape
    bs = x_sample.shape[0]
    (w_ada, b_ada, g_pre_mix, g_post_mix, g_pre_ffn, g_post_ffn, w_in, w_pool_map, pool_scale,
     w_proj_pool, w_proj_att, w_out, w_up, w_conv, b_conv, w_down) = (
        w[0] for w in (w_ada, b_ada, g_pre_mix, g_post_mix, g_pre_ffn, g_post_ffn, w_in, w_pool_map,
                       pool_scale, w_proj_pool, w_proj_att, w_out, w_up, w_conv, b_conv, w_down))
    caches = [(cache_k_w128[0], cache_v_w128[0]), (cache_k_w512[0], cache_v_w512[0]),
              (cache_k_w2048[0], cache_v_w2048[0])]
    g_pre_mix, g_post_mix, g_pre_ffn, g_post_ffn, pool_scale, b_conv, b_ada = (
        a.reshape(1, -1) for a in (g_pre_mix, g_post_mix, g_pre_ffn, g_post_ffn, pool_scale, b_conv, b_ada))
    w_in_b, w_map_b, w_pp_b, w_pa_b, w_out_b, w_up_b, w_down_b = (
        w.astype(BF16) for w in (w_in, w_pool_map, w_proj_pool, w_proj_att, w_out, w_up, w_down))

    mod = _ada(jnp.concatenate([c_prompt, c_sample], axis=0), w_ada, b_ada)
    mod_p = mod[:bp].reshape(bp, 6, d)
    mod_s = mod[bp:]

    proj_s, heads = _sfront(x_sample[:, 0], mod_s, g_pre_mix, w_in_b)
    q_s, k_s, v_s = (heads[j * N_GROUPS:(j + 1) * N_GROUPS] for j in range(3))
    cache_ops = [(q_s[g], k_s[g], v_s[g], kc.transpose(0, 2, 3, 1), vc.transpose(0, 2, 3, 1), dil)
                 for g, ((_, dil), (kc, vc)) in enumerate(zip(ATT_GROUPS, caches))]

    front = _front(x_prompt, mod_p, g_pre_mix, w_in_b, w_map_b, pool_scale, w_pp_b, cache_ops[:2])
    gp, sg = front[0], front[1]
    qkv = front[2:11]
    kv_t = front[11:17]
    u_tail = front[17]
    prompt_units = [(qkv[3 * g], qkv[3 * g + 1], qkv[3 * g + 2], dil) for g, (_, dil) in enumerate(ATT_GROUPS)]

    res = _sattn(*cache_ops[2], prompt_units)
    o_list, lse_list = list(res[N_SATTN_OUT::2]), list(res[N_SATTN_OUT + 1::2])
    cache_res = [front[N_FRONT_OUT:N_FRONT_OUT + N_SATTN_OUT], front[N_FRONT_OUT + N_SATTN_OUT:],
                 res[:N_SATTN_OUT]]
    o_s, lse_s, new_caches = [], [], []
    for o_g, lse_g, kt_new, vt_new in cache_res:
        o_s.append(o_g.reshape(bs, D_GROUP))
        lse_s.append(jnp.pad(lse_g[:, :, 0], ((0, 0), (0, LANES - N_SLOTS))))
        new_caches += [kt_new.transpose(0, 3, 1, 2)[None], vt_new.transpose(0, 3, 1, 2)[None]]
    hist_t = state_pool[0].transpose(1, 0, 2)
    chist_t = state_conv[0].transpose(1, 0, 2)
    pool_state_t = _pool_state(hist_t, proj_s)
    y_s, hu_s = _sback(
        x_sample[:, 0], mod_s, proj_s, hist_t, o_s, lse_s, chist_t,
        (w_map_b, pool_scale, w_pp_b, w_pa_b, w_out_b, w_up_b, w_conv, b_conv, w_down_b),
        (g_post_mix, g_pre_ffn, g_post_ffn))
    conv_state_s = jnp.concatenate([state_conv[0][:, 1:], hu_s[:, None]], axis=1)

    y_p, conv_tail = _back(x_prompt, mod_p, gp, sg, o_list, lse_list, w_pa_b, w_out_b, w_up_b, w_conv,
                           b_conv, w_down_b, g_post_mix, g_pre_ffn, g_post_ffn)

    outs = [y_p, y_s[:, None]]
    for g in range(N_GROUPS):
        for j in range(2):
            f = kv_t[2 * g + j]
            outs.append(f.reshape(bp, N_SLOTS, HEAD_DIM, f.shape[2]).transpose(0, 3, 1, 2)[None])
            outs.append(new_caches[2 * g + j])
    outs += [u_tail[None, :, HALO - POOL_HIST:], pool_state_t.transpose(1, 0, 2)[None],
             conv_tail[None, :, CONV_HALO - (CONV_W - 1):], conv_state_s[None]]
    return tuple(outs)
```

```python
import functools

import jax
import jax.numpy as jnp
from jax import lax
from jax.experimental import pallas as pl
from jax.experimental.pallas import tpu as pltpu

F32 = jnp.float32
BF16 = jnp.bfloat16

D_MODEL = 1024
EPS = 1e-6
POOL_WINDOWS = (2, 4, 8, 16)
POOL_GROUP = D_MODEL // len(POOL_WINDOWS)
POOL_HIST = max(POOL_WINDOWS) - 1
ATT_GROUPS = ((128, 1), (512, 4), (2048, 16))
SPAN = 128
N_GROUPS = len(ATT_GROUPS)
N_SLOTS = 8
HEAD_DIM = 64
D_GROUP = N_SLOTS * HEAD_DIM
D_ATT = N_GROUPS * D_GROUP
D_FF = 2816
CONV_W = 3
COL_U, COL_Q, COL_K, COL_V, COL_ZP, COL_ZA, D_IN = 0, 1024, 2560, 4096, 5632, 6656, 7680
Q_SCALE = HEAD_DIM ** -0.5
NEG = -1e30

LANES = 128
GROUP_SLABS = D_GROUP // LANES
HALO = 16
CONV_HALO = 8
TM = 256
TM_BACK = 512
FFN_ROWS = 512
FFN_CHUNKS = 1
VMEM_LIMIT = 62 * 1024 * 1024
CACHE_BLOCK_BYTES = 4 * 1024 * 1024
N_SATTN_IN, N_SATTN_OUT = 8, 4


def _const_spec(shape):
    nd = len(shape)
    return pl.BlockSpec(shape, lambda *_: (0,) * nd, pipeline_mode=pl.Buffered(1))


def _rms(x, g):
    return x * lax.rsqrt(jnp.mean(x * x, axis=-1, keepdims=True) + EPS) * g


def _sigmoid(x):
    return 1.0 / (1.0 + jnp.exp(-x))


def _gelu_tanh(x):
    half = 0.5 * x
    return half + half * jnp.tanh(x * (0.7978845608028654 + (0.7978845608028654 * 0.044715) * (x * x)))


def _dot(a, b):
    return jnp.dot(a, b, preferred_element_type=F32)


def _params(*semantics):
    return pltpu.CompilerParams(dimension_semantics=semantics, vmem_limit_bytes=VMEM_LIMIT)


def _ada_kernel(c_ref, w_ref, b_ref, o_ref):
    c = c_ref[...]
    s = c * _sigmoid(c)
    o_ref[...] = _dot(s.astype(BF16), w_ref[...].astype(BF16)) + b_ref[...]


def _ada(c_all, w_ada, b_ada):
    m = c_all.shape[0]
    n = w_ada.shape[1]
    tn = 1536
    return pl.pallas_call(
        _ada_kernel,
        grid=(n // tn,),
        in_specs=[pl.BlockSpec((m, D_MODEL), lambda j: (0, 0)),
                  pl.BlockSpec((D_MODEL, tn), lambda j: (0, j)),
                  pl.BlockSpec((1, tn), lambda j: (0, j))],
        out_specs=pl.BlockSpec((m, tn), lambda j: (0, j)),
        out_shape=jax.ShapeDtypeStruct((m, n), F32),
        compiler_params=_params("arbitrary"),
        name="ada",
    )(c_all, w_ada, b_ada)


N_FRONT_IN, N_FRONT_OUT = 7, 18


def _front_kernel(*refs, cache_dils):
    n_in = N_FRONT_IN + N_SATTN_IN * len(cache_dils)
    x_ref, mod_ref, g_ref, win_ref, wmap_ref, pscale_ref, wpp_ref = refs[:N_FRONT_IN]
    (gp_ref, sg_ref, q0, k0, v0, q1, k1, v1, q2, k2, v2,
     kf0, vf0, kf1, vf1, kf2, vf2, utail_ref) = refs[n_in:n_in + N_FRONT_OUT]
    ubuf, slabs = refs[-2:]
    i = pl.program_id(1)
    tm = x_ref.shape[1]

    @pl.when((pl.program_id(0) == 0) & (i == 0))
    def _():
        ubuf[tm:tm + HALO, :] = jnp.zeros((HALO, D_MODEL), F32)

    hosted = []
    for c, cache_dil in enumerate(cache_dils):
        c_in = refs[N_FRONT_IN + c * N_SATTN_IN:N_FRONT_IN + (c + 1) * N_SATTN_IN]
        c_out = refs[n_in + N_FRONT_OUT + c * N_SATTN_OUT:n_in + N_FRONT_OUT + (c + 1) * N_SATTN_OUT]
        hosted.append((range(c_in[-1].shape[0]), _cache_update(c_in, c_out, cache_dil)))
    for rows, (c_scores, _, _) in hosted:
        for b in rows:
            c_scores(b)

    x = x_ref[0]
    shift = mod_ref[0, 0:1, :]
    scale = mod_ref[0, 1:2, :]
    h = (_rms(x, g_ref[...]) * (1.0 + scale) + shift).astype(BF16)

    u = _dot(h, win_ref[:, COL_U:COL_Q])
    ubuf[0:HALO, :] = jnp.where(i > 0, ubuf[tm:tm + HALO, :], 0.0)
    ubuf[HALO:HALO + tm, :] = u
    utail_ref[0] = u[tm - HALO:, :]

    pos = i * tm + lax.broadcasted_iota(jnp.int32, (tm, 1), 0)
    pooled = []
    for g, win in enumerate(POOL_WINDOWS):
        lo, hi = g * POOL_GROUP, (g + 1) * POOL_GROUP
        wsum = ubuf[HALO:HALO + tm, lo:hi]
        for j in range(1, win):
            wsum = wsum + ubuf[HALO - j:HALO - j + tm, lo:hi]
        count = jnp.minimum(pos + 1, win).astype(F32)
        pooled.append((wsum / count - ubuf[HALO:HALO + tm, lo:hi]).astype(BF16))

    sg_ref[0] = _sigmoid(_dot(h, win_ref[:, COL_ZA:D_IN])).astype(BF16)
    pool_gate = _sigmoid(_dot(h, win_ref[:, COL_ZP:COL_ZA]))

    def emit(val, refs, frefs):
        for j in range(D_ATT // LANES):
            slabs[j] = val[:, j * LANES:(j + 1) * LANES]
        for g, (_, dil) in enumerate(ATT_GROUPS):
            if dil == 1:
                refs[g][0, 0] = val[:, g * D_GROUP:(g + 1) * D_GROUP].astype(BF16)
                continue
            for r in range(dil):
                for jj in range(GROUP_SLABS):
                    piece = slabs[g * GROUP_SLABS + jj, pl.ds(r, tm // dil, stride=dil), :]
                    refs[g][0, r, :, jj * LANES:(jj + 1) * LANES] = piece.astype(BF16)
        for g, fref in enumerate(frefs):
            rows = fref.shape[2]
            fref[0] = val[tm - rows:, g * D_GROUP:(g + 1) * D_GROUP].T

    emit(_dot(h, win_ref[:, COL_Q:COL_K]) * Q_SCALE, (q0, q1, q2), ())
    for rows, (_, c_softmax, c_values) in hosted:
        for b in rows:
            c_softmax(b)
        for b in rows:
            c_values(b)
    emit(_dot(h, win_ref[:, COL_K:COL_V]), (k0, k1, k2), (kf0, kf1, kf2))
    emit(_dot(h, win_ref[:, COL_V:COL_ZP]), (v0, v1, v2), (vf0, vf1, vf2))

    mixed = [_dot(p, wmap_ref[g]) for g, p in enumerate(pooled)]
    mixed = jnp.concatenate(mixed, axis=-1) * pscale_ref[...]
    y_pool = _dot(mixed.astype(BF16), wpp_ref[...])
    gp_ref[0] = (pool_gate * y_pool).astype(BF16)


def _hosted_caches(caches, steps_per_batch, n_steps):
    args, ins, outs, shapes, dils = [], [], [], [], []
    for q, kn, vn, kt, vt, dil in caches:
        bb = kt.shape[0] // n_steps
        assert bb * n_steps == kt.shape[0]
        ops = _cache_operands(q, kn, vn, kt, vt, bb, lambda bi, i: bi * steps_per_batch + i)
        for acc, new in zip((args, ins, outs, shapes), ops):
            acc += new
        dils.append(dil)
    return args, ins, outs, shapes, tuple(dils)


def _front(x, mod, g_pre, w_in, w_map, pool_scale, w_pp, caches=()):
    b, t, d = x.shape
    tm = TM
    nblk = t // tm
    c_args, c_ins, c_outs, c_shapes, cache_dils = _hosted_caches(caches, nblk, b * nblk)
    row = lambda shape: pl.BlockSpec(shape, lambda bi, i: (bi, i, 0))
    in_specs = [row((1, tm, d)),
                pl.BlockSpec((1, 6, d), lambda bi, i: (bi, 0, 0)),
                _const_spec((1, d)), _const_spec(w_in.shape), _const_spec(w_map.shape),
                _const_spec((1, d)), _const_spec(w_pp.shape)]
    out_shape = [jax.ShapeDtypeStruct((b, t, d), BF16), jax.ShapeDtypeStruct((b, t, d), BF16)]
    out_specs = [row((1, tm, d)), row((1, tm, d))]
    for _, dil in ATT_GROUPS:
        for _ in range(3):
            out_shape.append(jax.ShapeDtypeStruct((b, dil, t // dil, D_GROUP), BF16))
            out_specs.append(pl.BlockSpec((1, dil, tm // dil, D_GROUP), lambda bi, i: (bi, 0, i, 0)))
    for win, _ in ATT_GROUPS:
        keep = min(win, t)
        rows = min(keep, tm)
        first = nblk - keep // rows
        for _ in range(2):
            out_shape.append(jax.ShapeDtypeStruct((b, D_GROUP, keep), F32))
            out_specs.append(pl.BlockSpec(
                (1, D_GROUP, rows), lambda bi, i, first=first: (bi, 0, jnp.maximum(i - first, 0))))
    out_shape.append(jax.ShapeDtypeStruct((b, HALO, d), F32))
    out_specs.append(pl.BlockSpec((1, HALO, d), lambda bi, i: (bi, 0, 0)))
    assert (len(in_specs), len(out_specs)) == (N_FRONT_IN, N_FRONT_OUT)
    return pl.pallas_call(
        functools.partial(_front_kernel, cache_dils=cache_dils),
        grid=(b, nblk),
        in_specs=in_specs + c_ins, out_specs=out_specs + c_outs, out_shape=out_shape + c_shapes,
        scratch_shapes=[pltpu.VMEM((HALO + tm, d), F32),
                        pltpu.VMEM((D_ATT // LANES, tm, LANES), F32)],
        compiler_params=_params("arbitrary", "arbitrary"),
        name="front",
    )(x, mod, g_pre, w_in, w_map, pool_scale, w_pp, *c_args)


def _attn_unit_tasks(q_ref, kp_ref, kc_ref, vp_ref, vc_ref, o_ref, lse_ref, n, r, dil, has_prev):
    s2 = 2 * SPAN
    lane = lax.broadcasted_iota(jnp.int32, (SPAN, LANES), 1)
    low = lane < HEAD_DIM
    rows = pl.ds(r, SPAN, stride=dil) if dil > 1 else slice(None)
    state = {"lse": jnp.zeros((SPAN, LANES), F32)}

    def pair(j):
        n_keys = s2 if has_prev else SPAN
        row = lax.broadcasted_iota(jnp.int32, (s2, n_keys), 0) % SPAN
        col = lax.broadcasted_iota(jnp.int32, (s2, n_keys), 1)
        sl = slice(j * LANES, (j + 1) * LANES)
        if has_prev:
            first_row = row + jnp.where(n > 0, 0, s2)
            mask = jnp.where(col < SPAN, col - first_row, row - (col - SPAN)) >= 0
            kk = jnp.concatenate([kp_ref[:, sl], kc_ref[:, sl]], axis=0)
            vv = jnp.concatenate([vp_ref[:, sl], vc_ref[:, sl]], axis=0)
        else:
            mask = col <= row
            kk, vv = kc_ref[:, sl], vc_ref[:, sl]
        q2 = q_ref[:, sl]
        zero = jnp.zeros_like(q2)
        qs = jnp.concatenate([jnp.where(low, q2, zero), jnp.where(low, zero, q2)], axis=0)
        s = lax.dot_general(qs, kk, (((1,), (1,)), ((), ())), preferred_element_type=F32)
        s = jnp.where(mask, s, NEG)
        m = jnp.max(s, axis=-1, keepdims=True)
        p = jnp.exp(s - m)
        den = jnp.sum(p, axis=-1, keepdims=True)
        o2 = _dot(p.astype(BF16), vv) / den
        lse2 = m + jnp.log(den)
        o_ref[j, rows, :] = jnp.where(low, o2[:SPAN], o2[SPAN:])
        tile = jnp.where(lane == 2 * j, lse2[:SPAN], state["lse"])
        state["lse"] = jnp.where(lane == 2 * j + 1, lse2[SPAN:], tile)

    def finish():
        lse_ref[rows, :] = state["lse"]

    return [functools.partial(pair, j) for j in range(N_SLOTS // 2)] + [finish]


def _attn_unit_specs(q, dil):
    b, _, n_sub, _ = q.shape
    nb = n_sub // SPAN
    t = n_sub * dil
    unit = lambda i: (i // (nb * dil), (i // dil) % nb, i % dil)

    def cur(i):
        bi, n, r = unit(i)
        return bi, r, n, 0

    def prev(i):
        bi, n, r = unit(i)
        return bi, r, jnp.maximum(n - 1, 0), 0

    blk = lambda f: pl.BlockSpec((None, None, SPAN, D_GROUP), f)
    in_specs = [blk(cur), blk(prev), blk(cur), blk(prev), blk(cur)]
    out_specs = [pl.BlockSpec((None, GROUP_SLABS, SPAN * dil, LANES), lambda i: (unit(i)[0], 0, unit(i)[1], 0)),
                 pl.BlockSpec((None, SPAN * dil, LANES), lambda i: (unit(i)[0], unit(i)[1], 0))]
    out_shape = [jax.ShapeDtypeStruct((b, GROUP_SLABS, t, LANES), F32),
                 jax.ShapeDtypeStruct((b, t, LANES), F32)]
    return b * nb * dil, nb, in_specs, out_specs, out_shape


def _merge_groups(o_list, lse_list):
    r = lax.broadcasted_iota(jnp.int32, (LANES, D_GROUP), 0)
    c = lax.broadcasted_iota(jnp.int32, (LANES, D_GROUP), 1)
    expand = jnp.where(c // HEAD_DIM == r, 1.0, 0.0).astype(BF16)
    top = jnp.maximum(jnp.maximum(lse_list[0], lse_list[1]), lse_list[2])
    e = [jnp.exp(l - top) for l in lse_list]
    den = e[0] + e[1] + e[2]
    spread = []
    for eg in e[:-1]:
        w = eg / den
        hi = w.astype(BF16)
        lo = (w - hi.astype(F32)).astype(BF16)
        spread.append(_dot(hi, expand) + _dot(lo, expand))
    spread.append(1.0 - spread[0] - spread[1])
    return spread[0] * o_list[0] + spread[1] * o_list[1] + spread[2] * o_list[2]


def _gated_ffn(project, conv_taps, wconv_ref, bconv_ref, wdown_ref, row_groups, n_chunks):
    fc = D_FF // n_chunks
    cols = lambda c: (slice(c * fc, (c + 1) * fc), slice(D_FF + c * fc, D_FF + (c + 1) * fc))

    def up(r, c):
        for cs in cols(c):
            project(r, cs)

    def gate(r, c):
        halves = []
        for cs in cols(c):
            hc = bconv_ref[:, cs]
            for j in range(CONV_W):
                hc = hc + conv_taps(j, r, cs) * wconv_ref[j:j + 1, cs]
            halves.append(hc)
        return (_gelu_tanh(halves[0]) * halves[1]).astype(BF16)

    def down(acc, gated, c):
        part = _dot(gated, wdown_ref[c * fc:(c + 1) * fc, :])
        return part if acc is None else acc + part

    for c in range(n_chunks):
        up(0, c)
    out, gated, acc_prev = [], {}, None
    for r in range(row_groups):
        for c in range(n_chunks):
            if r + 1 < row_groups:
                up(r + 1, c)
            if r > 0:
                acc_prev = down(acc_prev, gated.pop((r - 1, c)), c)
            gated[r, c] = gate(r, c)
        if r > 0:
            out.append(acc_prev)
        acc_prev = None
    for c in range(n_chunks):
        acc_prev = down(acc_prev, gated.pop((row_groups - 1, c)), c)
    out.append(acc_prev)
    return out[0] if row_groups == 1 else jnp.concatenate(out, axis=0)


def _back_kernel(x_ref, mod_ref, gp_ref, sg_ref, o0, o1, o2, l0, l1, l2,
                 wpa_ref, wout_ref, wup_ref, wconv_ref, bconv_ref, wdown_ref,
                 gpost_ref, gpre_ref, gffn_ref, y_ref, ctail_ref, *hbufs):
    i = pl.program_id(1)
    n = hbufs[0].shape[0] - CONV_HALO

    @pl.when((pl.program_id(0) == 0) & (i == 0))
    def _():
        hbufs[-1][n:n + CONV_HALO, :] = jnp.zeros((CONV_HALO, 2 * D_FF), F32)

    mod = lambda r: mod_ref[0, r:r + 1, :]
    o_list = [jnp.concatenate([ref[j] for j in range(GROUP_SLABS)], axis=-1) for ref in (o0, o1, o2)]
    o = _merge_groups(o_list, [l0[0], l1[0], l2[0]])
    y_att = _dot(o.astype(BF16), wpa_ref[...])
    mix = gp_ref[0].astype(F32) + sg_ref[0].astype(F32) * y_att
    x1 = x_ref[0] + mod(2) * _rms(_dot(mix.astype(BF16), wout_ref[...]), gpost_ref[...])
    h2 = (_rms(x1, gpre_ref[...]) * (1.0 + mod(4)) + mod(3)).astype(BF16)

    hbufs[0][0:CONV_HALO, :] = jnp.where(i > 0, hbufs[-1][n:n + CONV_HALO, :], 0.0)

    def project(r, cs):
        up = _dot(h2[r * n:(r + 1) * n], wup_ref[:, cs])
        hbufs[r][CONV_HALO:CONV_HALO + n, cs] = up
        if r + 1 < len(hbufs):
            hbufs[r + 1][0:CONV_HALO, cs] = up[n - CONV_HALO:]

    def taps(j, r, cs):
        off = CONV_HALO - (CONV_W - 1) + j
        return hbufs[r][off:off + n, cs]

    ffn = _gated_ffn(project, taps, wconv_ref, bconv_ref, wdown_ref, len(hbufs), FFN_CHUNKS)
    y_ref[0] = x1 + mod(5) * _rms(ffn, gffn_ref[...])
    ctail_ref[0] = hbufs[-1][n:n + CONV_HALO, :]


def _back(x, mod, gp, sg, o_list, lse_list, w_pa, w_out, w_up, w_conv, b_conv, w_down,
          g_post, g_pre, g_ffn):
    b, t, d = x.shape
    tm = TM_BACK
    row = lambda shape: pl.BlockSpec(shape, lambda bi, i: (bi, i, 0))
    slab = pl.BlockSpec((None, GROUP_SLABS, tm, LANES), lambda bi, i: (bi, 0, i, 0))
    in_specs = ([row((1, tm, d)), pl.BlockSpec((1, 6, d), lambda bi, i: (bi, 0, 0)),
                 row((1, tm, d)), row((1, tm, d))]
                + [slab] * N_GROUPS + [row((1, tm, LANES))] * N_GROUPS
                + [_const_spec(w.shape) for w in (w_pa, w_out, w_up, w_conv, b_conv, w_down)]
                + [_const_spec((1, d))] * 3)
    return pl.pallas_call(
        _back_kernel,
        grid=(b, t // tm),
        in_specs=in_specs,
        out_specs=[row((1, tm, d)), pl.BlockSpec((1, CONV_HALO, 2 * D_FF), lambda bi, i: (bi, 0, 0))],
        out_shape=[jax.ShapeDtypeStruct((b, t, d), F32),
                   jax.ShapeDtypeStruct((b, CONV_HALO, 2 * D_FF), F32)],
        scratch_shapes=[pltpu.VMEM((CONV_HALO + FFN_ROWS, 2 * D_FF), F32)] * (tm // FFN_ROWS),
        compiler_params=_params("arbitrary", "arbitrary"),
        name="back",
    )(x, mod, gp, sg, *o_list, *lse_list, w_pa, w_out, w_up, w_conv, b_conv, w_down,
      g_post, g_pre, g_ffn)


def _sfront_kernel(x_ref, shift_ref, scale_ref, g_ref, w_ref, o_ref, h_scr):
    @pl.when(pl.program_id(0) == 0)
    def _():
        h = _rms(x_ref[...], g_ref[...]) * (1.0 + scale_ref[...]) + shift_ref[...]
        h_scr[...] = h.astype(BF16)

    o_ref[...] = _dot(h_scr[...], w_ref[...])


def _sfront(x, mod, g_pre, w_in):
    m, d = x.shape
    n = w_in.shape[1]
    tn = 1536
    return pl.pallas_call(
        _sfront_kernel,
        grid=(n // tn,),
        in_specs=[pl.BlockSpec((m, d), lambda j: (0, 0)),
                  pl.BlockSpec((m, d), lambda j: (0, 0)),
                  pl.BlockSpec((m, d), lambda j: (0, 1)),
                  pl.BlockSpec((1, d), lambda j: (0, 0)),
                  pl.BlockSpec((d, tn), lambda j: (0, j))],
        out_specs=pl.BlockSpec((m, tn), lambda j: (0, j)),
        out_shape=jax.ShapeDtypeStruct((m, n), F32),
        scratch_shapes=[pltpu.VMEM((m, d), BF16)],
        compiler_params=_params("arbitrary"),
        name="sfront",
    )(x, mod, mod, g_pre, w_in)


def _cache_update(in_refs, out_refs, dil):
    qrow_ref, q_ref, kn_ref, vn_ref, knt_ref, vnt_ref, kt_ref, vt_ref = in_refs
    o_ref, lse_ref, kto_ref, vto_ref = out_refs
    bb, n_slots, dh, length = kt_ref.shape
    lane = lax.broadcasted_iota(jnp.int32, (1, length), 1)
    attended = (lane & (dil - 1)) == 0
    last = lane == length - 1
    slot = lax.broadcasted_iota(jnp.int32, (n_slots, 1), 0)
    state = [dict(sc=jnp.zeros((n_slots, length), F32), o=jnp.zeros((n_slots, dh), F32)) for _ in range(bb)]

    def score_slot(b, s):
        kt = kt_ref[b, s]
        state[b]["sc"] = state[b]["sc"] + _dot(qrow_ref[b, s], kt.astype(BF16))
        kto_ref[b, s] = jnp.where(last, knt_ref[b, :, s:s + 1], pltpu.roll(kt, length - 1, 1))

    def softmax(b):
        sc_new = jnp.sum(kn_ref[b] * q_ref[b], axis=-1, keepdims=True) * Q_SCALE
        sc = jnp.where(attended, state[b]["sc"], NEG)
        m = jnp.maximum(jnp.max(sc, axis=-1, keepdims=True), sc_new)
        p = jnp.exp(sc - m)
        p_new = jnp.exp(sc_new - m)
        den = jnp.sum(p, axis=-1, keepdims=True) + p_new
        lse_ref[b] = m + jnp.log(den)
        state[b].update(p=p.astype(BF16), p_new=p_new, inv=1.0 / den)

    def value_slot(b, s):
        vt = vt_ref[b, s]
        pv = lax.dot_general(state[b]["p"], vt.astype(BF16), (((1,), (1,)), ((), ())),
                             preferred_element_type=F32)
        state[b]["o"] = jnp.where(slot == s, pv, state[b]["o"])
        vto_ref[b, s] = jnp.where(last, vnt_ref[b, :, s:s + 1], pltpu.roll(vt, length - 1, 1))

    def scores(b):
        for s in range(n_slots):
            score_slot(b, s)

    def values(b):
        for s in range(n_slots):
            value_slot(b, s)
        o_ref[b] = (state[b]["o"] + vn_ref[b] * state[b]["p_new"]) * state[b]["inv"]

    return scores, softmax, values


def _cache_operands(q, kn, vn, kt, vt, bb, block):
    b, n_slots, dh, length = kt.shape
    own_row = jnp.eye(n_slots, dtype=F32)[None, :, :, None]
    qrow = (own_row * (q * Q_SCALE)[:, :, None, :]).astype(BF16)
    spec = lambda *shape: pl.BlockSpec((bb,) + shape, lambda *ids: (block(*ids),) + (0,) * len(shape))
    new, column, cache = spec(n_slots, dh), spec(dh, n_slots), spec(n_slots, dh, length)
    in_specs = [spec(n_slots, n_slots, dh), new, new, new, column, column, cache, cache]
    assert len(in_specs) == N_SATTN_IN
    out_specs = [new, spec(n_slots, 1), cache, cache]
    out_shape = [jax.ShapeDtypeStruct((b, n_slots, dh), F32),
                 jax.ShapeDtypeStruct((b, n_slots, 1), F32),
                 jax.ShapeDtypeStruct(kt.shape, F32), jax.ShapeDtypeStruct(vt.shape, F32)]
    args = [qrow, q, kn, vn, kn.transpose(0, 2, 1), vn.transpose(0, 2, 1), kt, vt]
    return args, in_specs, out_specs, out_shape


def _sattn_kernel(*refs, dil, units):
    n_in = N_SATTN_IN + 5 * len(units)
    scores, softmax, values = _cache_update(refs[:N_SATTN_IN], refs[n_in:n_in + N_SATTN_OUT], dil)
    for b in range(refs[N_SATTN_IN - 1].shape[0]):
        scores(b)
        softmax(b)
        values(b)
    step = pl.program_id(0)
    for g, (nb, unit_dil) in enumerate(units):
        ins = refs[N_SATTN_IN + 5 * g:N_SATTN_IN + 5 * g + 5]
        outs = refs[n_in + N_SATTN_OUT + 2 * g:n_in + N_SATTN_OUT + 2 * g + 2]
        for task in _attn_unit_tasks(*ins, *outs, (step // unit_dil) % nb, step % unit_dil, unit_dil, nb > 1):
            task()


def _sattn(q, kn, vn, kt, vt, dil, prompt=()):
    b, n_slots, dh, length = kt.shape
    bb = max(1, CACHE_BLOCK_BYTES // (n_slots * dh * length * 4))
    steps = b // bb
    args, in_specs, out_specs, out_shape = _cache_operands(q, kn, vn, kt, vt, bb, lambda i: i)
    units = []
    for pq, pk, pv, pdil in prompt:
        n_units, nb, ins, outs, shapes = _attn_unit_specs(pq, pdil)
        assert n_units == steps, (n_units, steps)
        units.append((nb, pdil))
        in_specs += ins
        out_specs += outs
        out_shape += shapes
        args += [pq, pk, pk, pv, pv]
    return pl.pallas_call(
        functools.partial(_sattn_kernel, dil=dil, units=tuple(units)),
        grid=(steps,),
        in_specs=in_specs, out_specs=out_specs, out_shape=out_shape,
        compiler_params=_params("arbitrary"),
        name="sattn",
    )(*args)


def _pool_state_kernel(hist_ref, u_ref, o_ref):
    newest = pl.program_id(0) == POOL_HIST - 1
    o_ref[0] = jnp.where(newest, u_ref[...], hist_ref[0])


def _pool_state(hist_t, proj):
    steps, m, d = hist_t.shape
    return pl.pallas_call(
        _pool_state_kernel,
        grid=(steps,),
        in_specs=[pl.BlockSpec((1, m, d), lambda j: (jnp.minimum(j + 1, steps - 1), 0, 0)),
                  pl.BlockSpec((m, d), lambda j: (0, COL_U // d))],
        out_specs=pl.BlockSpec((1, m, d), lambda j: (j, 0, 0)),
        out_shape=jax.ShapeDtypeStruct(hist_t.shape, F32),
        compiler_params=_params("arbitrary"),
        name="pool_state",
    )(hist_t, proj)


def _sback_kernel(x_ref, mod_ref, proj_ref, hist_ref, o0, o1, o2, l0, l1, l2, chist_ref,
                  wmap_ref, pscale_ref, wpp_ref, wpa_ref, wout_ref, wup_ref, wconv_ref, bconv_ref,
                  wdown_ref, gpost_ref, gpre_ref, gffn_ref, y_ref, hu_ref):
    m = x_ref.shape[0]
    mod = lambda r: mod_ref[:, r * D_MODEL:(r + 1) * D_MODEL]
    u = proj_ref[:, COL_U:COL_Q]
    mixed = []
    for g, win in enumerate(POOL_WINDOWS):
        lo, hi = g * POOL_GROUP, (g + 1) * POOL_GROUP
        wsum = u[:, lo:hi]
        for j in range(1, win):
            wsum = wsum + hist_ref[POOL_HIST - j, :, lo:hi]
        pooled = wsum / float(win) - u[:, lo:hi]
        mixed.append(_dot(pooled.astype(BF16), wmap_ref[g]))
    mixed = jnp.concatenate(mixed, axis=-1) * pscale_ref[...]
    y_pool = _dot(mixed.astype(BF16), wpp_ref[...])
    o = _merge_groups([o0[...], o1[...], o2[...]], [l0[...], l1[...], l2[...]])
    y_att = _dot(o.astype(BF16), wpa_ref[...])
    mix = (_sigmoid(proj_ref[:, COL_ZP:COL_ZA]) * y_pool + _sigmoid(proj_ref[:, COL_ZA:D_IN]) * y_att)
    x1 = x_ref[...] + mod(2) * _rms(_dot(mix.astype(BF16), wout_ref[...]), gpost_ref[...])
    h2 = (_rms(x1, gpre_ref[...]) * (1.0 + mod(4)) + mod(3)).astype(BF16)

    def project(r, cs):
        hu_ref[:, cs] = _dot(h2, wup_ref[:, cs])

    def taps(j, r, cs):
        return hu_ref[:, cs] if j == CONV_W - 1 else chist_ref[j, :, cs]

    ffn = _gated_ffn(project, taps, wconv_ref, bconv_ref, wdown_ref, 1, 1)
    y_ref[...] = x1 + mod(5) * _rms(ffn, gffn_ref[...])


def _sback(x, mod, proj, hist_t, o_list, lse_list, chist_t, weights, gains):
    m, d = x.shape
    args = [x, mod, proj, hist_t, *o_list, *lse_list, chist_t, *weights, *gains]
    full = lambda shape: pl.BlockSpec(shape, lambda i: (0,) * len(shape))
    out_shapes = [(m, d), (m, 2 * D_FF)]
    return pl.pallas_call(
        _sback_kernel,
        grid=(1,),
        in_specs=[_const_spec(a.shape) for a in args],
        out_specs=[full(s) for s in out_shapes],
        out_shape=[jax.ShapeDtypeStruct(s, F32) for s in out_shapes],
        compiler_params=_params("arbitrary"),
        name="sback",
    )(*args)


def kernel(x_prompt, x_sample, c_prompt, c_sample, cache_k_w128, cache_v_w128, cache_k_w512, cache_v_w512,
           cache_k_w2048, cache_v_w2048, state_pool, state_conv, w_ada, b_ada, g_pre_mix, g_post_mix,
           g_pre_ffn, g_post_ffn, w_in, w_pool_map, pool_scale, w_proj_pool, w_proj_att, w_out, w_up,
           w_conv, b_conv, w_down):
    bp, tp, d = x_prompt.shape
    bs = x_sample.shape[0]
    (w_ada, b_ada, g_pre_mix, g_post_mix, g_pre_ffn, g_post_ffn, w_in, w_pool_map, pool_scale,
     w_proj_pool, w_proj_att, w_out, w_up, w_conv, b_conv, w_down) = (
        w[0] for w in (w_ada, b_ada, g_pre_mix, g_post_mix, g_pre_ffn, g_post_ffn, w_in, w_pool_map,
                       pool_scale, w_proj_pool, w_proj_att, w_out, w_up, w_conv, b_conv, w_down))
    caches = [(cache_k_w128[0], cache_v_w128[0]), (cache_k_w512[0], cache_v_w512[0]),
              (cache_k_w2048[0], cache_v_w2048[0])]
    g_pre_mix, g_post_mix, g_pre_ffn, g_post_ffn, pool_scale, b_conv, b_ada = (
        a.reshape(1, -1) for a in (g_pre_mix, g_post_mix, g_pre_ffn, g_post_ffn, pool_scale, b_conv, b_ada))
    w_in_b, w_map_b, w_pp_b, w_pa_b, w_out_b, w_up_b, w_down_b = (
        w.astype(BF16) for w in (w_in, w_pool_map, w_proj_pool, w_proj_att, w_out, w_up, w_down))

    mod = _ada(jnp.concatenate([c_prompt, c_sample], axis=0), w_ada, b_ada)
    mod_p = mod[:bp].reshape(bp, 6, d)
    mod_s = mod[bp:]

    proj_s = _sfront(x_sample[:, 0], mod_s, g_pre_mix, w_in_b)
    heads = lambda lo: proj_s[:, lo:lo + D_ATT].reshape(bs, N_GROUPS, N_SLOTS, HEAD_DIM).transpose(1, 0, 2, 3)
    q_s, k_s, v_s = heads(COL_Q), heads(COL_K), heads(COL_V)
    cache_ops = [(q_s[g], k_s[g], v_s[g], kc.transpose(0, 2, 3, 1), vc.transpose(0, 2, 3, 1), dil)
                 for g, ((_, dil), (kc, vc)) in enumerate(zip(ATT_GROUPS, caches))]

    front = _front(x_prompt, mod_p, g_pre_mix, w_in_b, w_map_b, pool_scale, w_pp_b, cache_ops[:2])
    gp, sg = front[0], front[1]
    qkv = front[2:11]
    kv_t = front[11:17]
    u_tail = front[17]
    prompt_units = [(qkv[3 * g], qkv[3 * g + 1], qkv[3 * g + 2], dil) for g, (_, dil) in enumerate(ATT_GROUPS)]

    res = _sattn(*cache_ops[2], prompt_units)
    o_list, lse_list = list(res[N_SATTN_OUT::2]), list(res[N_SATTN_OUT + 1::2])
    cache_res = [front[N_FRONT_OUT:N_FRONT_OUT + N_SATTN_OUT], front[N_FRONT_OUT + N_SATTN_OUT:],
                 res[:N_SATTN_OUT]]
    o_s, lse_s, new_caches = [], [], []
    for o_g, lse_g, kt_new, vt_new in cache_res:
        o_s.append(o_g.reshape(bs, D_GROUP))
        lse_s.append(jnp.pad(lse_g[:, :, 0], ((0, 0), (0, LANES - N_SLOTS))))
        new_caches += [kt_new.transpose(0, 3, 1, 2)[None], vt_new.transpose(0, 3, 1, 2)[None]]
    hist_t = state_pool[0].transpose(1, 0, 2)
    chist_t = state_conv[0].transpose(1, 0, 2)
    pool_state_t = _pool_state(hist_t, proj_s)
    y_s, hu_s = _sback(
        x_sample[:, 0], mod_s, proj_s, hist_t, o_s, lse_s, chist_t,
        (w_map_b, pool_scale, w_pp_b, w_pa_b, w_out_b, w_up_b, w_conv, b_conv, w_down_b),
        (g_post_mix, g_pre_ffn, g_post_ffn))
    conv_state_s = jnp.concatenate([state_conv[0][:, 1:], hu_s[:, None]], axis=1)

    y_p, conv_tail = _back(x_prompt, mod_p, gp, sg, o_list, lse_list, w_pa_b, w_out_b, w_up_b, w_conv,
                           b_conv, w_down_b, g_post_mix, g_pre_ffn, g_post_ffn)

    outs = [y_p, y_s[:, None]]
    for g in range(N_GROUPS):
        for j in range(2):
            f = kv_t[2 * g + j]
            outs.append(f.reshape(bp, N_SLOTS, HEAD_DIM, f.shape[2]).transpose(0, 3, 1, 2)[None])
            outs.append(new_caches[2 * g + j])
    outs += [u_tail[None, :, HALO - POOL_HIST:], pool_state_t.transpose(1, 0, 2)[None],
             conv_tail[None, :, CONV_HALO - (CONV_W - 1):], conv_state_s[None]]
    return tuple(outs)
```
